```python
import math
import jax, jax.numpy as jnp
from jax import lax
import numpy as np

D_MODEL = 2048
BATCH = 8
SEQ = 2048
DEPTH = 2
DEC_BATCH = 128
DEC_SEQ = 8
PAST_LEN = 2048
PAGE_SIZE = 128

N_A_LAYERS = DEPTH // 2
N_B_LAYERS = DEPTH - N_A_LAYERS
HGRN_HEADS = D_MODEL // 128
HGRN_DK = D_MODEL // HGRN_HEADS
HGRN_DV = D_MODEL // HGRN_HEADS
HGRN_CHUNK = 64
SB_HEADS = D_MODEL // 128
SB_HEAD_DIM = D_MODEL // SB_HEADS
SB_BLOCK = 128
SB_SCALE = SB_HEAD_DIM ** -0.5
SB_BIAS_INIT = -8.0
D_FF = 256 * ((8 * D_MODEL // 3 + 255) // 256)
CONV_WIDTH = 3
PLE_DIM = 256
EPS = 1e-6
F32 = jnp.float32

kernel_name = "yoco_hgrn2_stickbreaking_convffn_step"


def rms_norm(x, g):
    xf = x.astype(F32)
    y = xf * lax.rsqrt(jnp.mean(xf * xf, axis=-1, keepdims=True) + EPS)
    return (y * g.astype(F32)).astype(x.dtype)


def hgrn_recurrence(q, k, v, logf, s0):
    B, T, H, DK = q.shape
    DV = v.shape[-1]
    C = math.gcd(T, HGRN_CHUNK)
    n = T // C

    def to_chunks(a):
        return a.reshape(B, n, C, H, a.shape[-1]).transpose(1, 0, 3, 2, 4)

    tri = jnp.tril(jnp.ones((C, C), dtype=bool))[:, :, None]

    def step(S, xs):
        qc, kc, vc, gc = xs
        b = jnp.cumsum(gc, axis=2)
        diff = b[:, :, :, None, :] - b[:, :, None, :, :]
        dec = jnp.exp(jnp.where(tri, diff, -jnp.inf))
        att = jnp.einsum("bhtc,bhsc,bhtsc->bhts", qc, kc, dec)
        o = jnp.einsum("bhts,bhsv->bhtv", att, vc) + jnp.einsum("bhtc,bhcv->bhtv", qc * jnp.exp(b), S)
        b_last = b[:, :, -1:, :]
        S = jnp.exp(b_last[:, :, 0, :])[..., None] * S + jnp.einsum("bhsc,bhsv->bhcv", kc * jnp.exp(b_last - b), vc)
        return S, o

    s_final, o = lax.scan(step, s0.astype(F32), (to_chunks(q), to_chunks(k), to_chunks(v), to_chunks(logf)))
    o = o.transpose(1, 0, 3, 2, 4).reshape(B, T, H, DV)
    return o, s_final


def hgrn_mixer(h, w_in, lb, g_norm, w_out, s0):
    B, T, _ = h.shape
    q, fz, i, g = jnp.split(h @ w_in, 4, axis=-1)
    f = lb + (1.0 - lb) * jax.nn.sigmoid(fz.astype(F32))
    shp = (B, T, HGRN_HEADS, HGRN_DK)
    qh = jax.nn.silu(q.astype(F32)).reshape(shp)
    kh = (1.0 - f).reshape(shp)
    logf = jnp.log(f).reshape(shp)
    vh = i.astype(F32).reshape(B, T, HGRN_HEADS, HGRN_DV)
    o, s = hgrn_recurrence(qh, kh, vh, logf, s0)
    o = rms_norm(o, g_norm).reshape(B, T, D_MODEL).astype(h.dtype) * jax.nn.silu(g)
    return o @ w_out, s.astype(s0.dtype)


def sb_prompt(q, k, v, bias):
    B, T, H, Dh = q.shape
    QB = math.gcd(T, SB_BLOCK)
    nb = T // QB
    qb = (q.astype(F32) * SB_SCALE).reshape(B, nb, QB, H, Dh).transpose(1, 0, 2, 3, 4)
    kf = k.astype(F32)
    vf = v.astype(F32)
    bf = bias.astype(F32)[None, :, None, None]
    kpos = jnp.arange(T)

    def block(args):
        qi, start = args
        z = jnp.einsum("bqhd,bkhd->bhqk", qi, kf) + bf
        qpos = start + jnp.arange(QB)
        mask = kpos[None, :] < qpos[:, None]
        L = jnp.where(mask, jax.nn.log_sigmoid(-z), 0.0)
        logA = jax.nn.log_sigmoid(z) + lax.cumsum(L, axis=3, reverse=True) - L
        A = jnp.where(mask, jnp.exp(logA), 0.0)
        return jnp.einsum("bhqk,bkhd->bqhd", A, vf)

    o = lax.map(block, (qb, jnp.arange(nb) * QB))
    return o.transpose(1, 0, 2, 3, 4).reshape(B, T, H, Dh).astype(q.dtype)


def sb_sample(q, k_new, v_new, bias, cache_k, cache_v, page_table):
    Q = q.shape[1]
    qf = q.astype(F32) * SB_SCALE
    bf = bias.astype(F32)[None, :, None, None]
    z = jnp.einsum("bqhd,bkhd->bhqk", qf, k_new.astype(F32)) + bf
    mask = jnp.arange(Q)[None, :] < jnp.arange(Q)[:, None]
    L = jnp.where(mask, jax.nn.log_sigmoid(-z), 0.0)
    A = jnp.where(mask, jnp.exp(jax.nn.log_sigmoid(z) + lax.cumsum(L, axis=3, reverse=True) - L), 0.0)
    o0 = jnp.einsum("bhqk,bkhd->bqhd", A, v_new.astype(F32))
    lsum0 = jnp.sum(L, axis=-1)

    def page_step(carry, pidx):
        o_acc, lsum = carry
        kp = cache_k[pidx].astype(F32)
        vp = cache_v[pidx].astype(F32)
        zp = jnp.einsum("bqhd,bkhd->bhqk", qf, kp) + bf
        Lp = jax.nn.log_sigmoid(-zp)
        logA = jax.nn.log_sigmoid(zp) + lsum[..., None] + lax.cumsum(Lp, axis=3, reverse=True) - Lp
        o_acc = o_acc + jnp.einsum("bhqk,bkhd->bqhd", jnp.exp(logA), vp)
        return (o_acc, lsum + jnp.sum(Lp, axis=-1)), None

    (o, _), _ = lax.scan(page_step, (o0, lsum0), page_table.T[::-1])
    return o.astype(q.dtype)


def conv_ffn(h, w_in, conv_w, conv_b, w_out, buf):
    T = h.shape[1]
    u_g, u_u = jnp.split(h @ w_in, 2, axis=-1)
    ext = jnp.concatenate([buf.astype(u_g.dtype), u_g], axis=1)
    conv = conv_b
    for j in range(CONV_WIDTH):
        conv = conv + ext[:, j:j + T] * conv_w[j]
    y = (jax.nn.silu(conv) * u_u) @ w_out
    return y, ext[:, -(CONV_WIDTH - 1):]


def per_layer_embed(x, p, w_pe, norm_g, w_gate):
    gate = jax.nn.sigmoid(rms_norm(x, norm_g) @ w_gate)
    return x + (p @ w_pe) * gate


def run_trunk(x, p, hgrn_state, conv_state, attend, W):
    (norm_mix, norm_ffn, a_w_in, lb_all, a_g_norm, a_w_out, kv_norm, kv_w, k_norm,
     b_w_q, b_q_norm, b_sb_bias, b_w_out, ffn_w_in, ffn_conv_w, ffn_conv_b, ffn_w_out,
     ple_w, ple_norm, ple_w_gate) = W
    B, T, _ = x.shape
    new_s, new_c = [], []
    k_sh = v_sh = None
    for layer in range(DEPTH):
        h = rms_norm(x, norm_mix[layer])
        if layer < N_A_LAYERS:
            o, s = hgrn_mixer(h, a_w_in[layer], lb_all[layer], a_g_norm[layer], a_w_out[layer], hgrn_state[layer])
            new_s.append(s)
            x = x + o
        else:
            bi = layer - N_A_LAYERS
            q = rms_norm((h @ b_w_q[bi]).reshape(B, T, SB_HEADS, SB_HEAD_DIM), b_q_norm[bi])
            o = attend(q, k_sh, v_sh, b_sb_bias[bi]).reshape(B, T, D_MODEL)
            x = x + o @ b_w_out[bi]
        f, cbuf = conv_ffn(rms_norm(x, norm_ffn[layer]), ffn_w_in[layer], ffn_conv_w[layer],
                           ffn_conv_b[layer], ffn_w_out[layer], conv_state[layer])
        new_c.append(cbuf)
        x = x + f
        x = per_layer_embed(x, p[layer], ple_w[layer], ple_norm[layer], ple_w_gate[layer])
        if layer == N_A_LAYERS - 1:
            hk = rms_norm(x, kv_norm)
            kk, vv = jnp.split(hk @ kv_w, 2, axis=-1)
            k_sh = rms_norm(kk.reshape(B, T, SB_HEADS, SB_HEAD_DIM), k_norm)
            v_sh = vv.reshape(B, T, SB_HEADS, SB_HEAD_DIM)
    return x, k_sh, v_sh, jnp.stack(new_s), jnp.stack(new_c)


def setup_inputs(seed: int = 0) -> dict:
    key = jax.random.key(seed)
    ks = iter(list(jax.random.split(key, 32)))

    def nrm(shape, scale):
        return scale * jax.random.normal(next(ks), shape, F32)

    def gain(shape):
        return 1.0 + 0.05 * jax.random.normal(next(ks), shape, F32)

    n_pages = PAST_LEN // PAGE_SIZE
    n_used = DEC_BATCH * n_pages
    n_pool = n_used + max(1, n_used // 4)
    D = D_MODEL
    inp = {}
    inp["x_prompt"] = nrm((BATCH, SEQ, D), 1.0)
    inp["x_sample"] = nrm((DEC_BATCH, DEC_SEQ, D), 1.0)
    inp["p_prompt"] = nrm((DEPTH, BATCH, SEQ, PLE_DIM), 1.0)
    inp["p_sample"] = nrm((DEPTH, DEC_BATCH, DEC_SEQ, PLE_DIM), 1.0)
    inp["cache_k"] = nrm((n_pool, PAGE_SIZE, SB_HEADS, SB_HEAD_DIM), 1.0)
    inp["cache_v"] = nrm((n_pool, PAGE_SIZE, SB_HEADS, SB_HEAD_DIM), 1.0)
    inp["page_table"] = jax.random.permutation(next(ks), n_pool)[:n_used].reshape(DEC_BATCH, n_pages).astype(jnp.int32)
    inp["state_hgrn"] = nrm((N_A_LAYERS, DEC_BATCH, HGRN_HEADS, HGRN_DK, HGRN_DV), 1.0)
    inp["state_conv"] = nrm((DEPTH, DEC_BATCH, CONV_WIDTH - 1, D_FF), 1.0)
    inp["norm_mix"] = gain((DEPTH, D))
    inp["norm_ffn"] = gain((DEPTH, D))
    inp["a_w_in"] = nrm((N_A_LAYERS, D, 4 * D), D ** -0.5)
    inp["a_lb"] = nrm((N_A_LAYERS + 1, D), 0.5)
    inp["a_g_norm"] = gain((N_A_LAYERS, HGRN_DV))
    inp["a_w_out"] = nrm((N_A_LAYERS, D, D), D ** -0.5)
    inp["kv_norm"] = gain((D,))
    inp["kv_w"] = nrm((D, 2 * D), D ** -0.5)
    inp["k_norm"] = gain((SB_HEAD_DIM,))
    inp["b_w_q"] = nrm((N_B_LAYERS, D, D), D ** -0.5)
    inp["b_q_norm"] = gain((N_B_LAYERS, SB_HEAD_DIM))
    inp["b_sb_bias"] = SB_BIAS_INIT + nrm((N_B_LAYERS, SB_HEADS), 0.1)
    inp["b_w_out"] = nrm((N_B_LAYERS, D, D), D ** -0.5)
    inp["ffn_w_in"] = nrm((DEPTH, D, 2 * D_FF), D ** -0.5)
    inp["ffn_conv_w"] = nrm((DEPTH, CONV_WIDTH, D_FF), CONV_WIDTH ** -0.5)
    inp["ffn_conv_b"] = nrm((DEPTH, D_FF), 0.01)
    inp["ffn_w_out"] = nrm((DEPTH, D_FF, D), D_FF ** -0.5)
    inp["ple_w"] = nrm((DEPTH, PLE_DIM, D), PLE_DIM ** -0.5)
    inp["ple_norm"] = gain((DEPTH, D))
    inp["ple_w_gate"] = nrm((DEPTH, D, D), D ** -0.5)
    return inp


def reference(x_prompt, x_sample, p_prompt, p_sample, cache_k, cache_v, page_table, state_hgrn, state_conv,
              norm_mix, norm_ffn, a_w_in, a_lb, a_g_norm, a_w_out, kv_norm, kv_w, k_norm,
              b_w_q, b_q_norm, b_sb_bias, b_w_out, ffn_w_in, ffn_conv_w, ffn_conv_b, ffn_w_out,
              ple_w, ple_norm, ple_w_gate):
    lb_all = jnp.cumsum(jax.nn.softmax(a_lb.astype(F32), axis=0), axis=0)
    W = (norm_mix, norm_ffn, a_w_in, lb_all, a_g_norm, a_w_out, kv_norm, kv_w, k_norm,
         b_w_q, b_q_norm, b_sb_bias, b_w_out, ffn_w_in, ffn_conv_w, ffn_conv_b, ffn_w_out,
         ple_w, ple_norm, ple_w_gate)
    B = x_prompt.shape[0]
    zero_h = jnp.zeros((N_A_LAYERS, B, HGRN_HEADS, HGRN_DK, HGRN_DV), state_hgrn.dtype)
    zero_c = jnp.zeros((DEPTH, B, CONV_WIDTH - 1, D_FF), state_conv.dtype)
    y_prompt, k_prompt, v_prompt, hgrn_prompt, conv_prompt = run_trunk(
        x_prompt, p_prompt, zero_h, zero_c, sb_prompt, W)

    def attend_sample(q, k, v, bias):
        return sb_sample(q, k, v, bias, cache_k, cache_v, page_table)

    y_sample, k_sample, v_sample, hgrn_sample, conv_sample = run_trunk(
        x_sample, p_sample, state_hgrn, state_conv, attend_sample, W)
    return (y_prompt, y_sample, k_prompt, v_prompt, k_sample, v_sample, hgrn_prompt, hgrn_sample, conv_prompt, conv_sample)
```

```python
import functools

import jax
import jax.numpy as jnp
from jax import lax
from jax.experimental import pallas as pl
from jax.experimental.pallas import tpu as pltpu

F32 = jnp.float32
BF16 = jnp.bfloat16
EPS = 1e-6
HEAD_DIM = 128
HGRN_CHUNK = 64
HGRN_SUB = 16
SB_BLOCK = 256
SB_PAGES_PER_STEP = 4
V7X_VMEM_CAP = 56 * 1024 * 1024
VMEM_TEMP_MARGIN = 16 * 1024 * 1024

NT_DIMS = (((1,), (1,)), ((), ()))
TN_DIMS = (((0,), (0,)), ((), ()))


def _params(semantics, block_bytes, scratch_bytes=0):
    limit = min(2 * block_bytes + scratch_bytes + VMEM_TEMP_MARGIN, V7X_VMEM_CAP)
    return pltpu.CompilerParams(dimension_semantics=semantics, vmem_limit_bytes=limit)


def _nbytes(shape, dtype):
    n = jnp.dtype(dtype).itemsize
    for s in shape:
        n *= s
    return n


def _rms(x, g):
    return x * lax.rsqrt(jnp.mean(x * x, axis=-1, keepdims=True) + EPS) * g


def _silu(x):
    return x * jax.nn.sigmoid(x)


def _softplus(z):
    return jnp.maximum(z, 0.0) + jnp.log1p(jnp.exp(-jnp.abs(z)))


def _bf16_split(x):
    hi = x.astype(BF16)
    lo = (x - hi.astype(F32)).astype(BF16)
    return hi, lo


def _dot(a, b, dims=None):
    if dims is None:
        return jnp.dot(a, b, preferred_element_type=F32)
    return lax.dot_general(a, b, dims, preferred_element_type=F32)


def _norm_matmul_kernel(x_ref, g_ref, w_ref, *rest, head_norm):
    if head_norm:
        hg_ref, o_ref, xn_ref = rest
    else:
        o_ref, xn_ref = rest

    @pl.when(pl.program_id(1) == 0)
    def _():
        xn_ref[...] = _rms(x_ref[...], g_ref[...]).astype(BF16)

    acc = _dot(xn_ref[...], w_ref[...])
    if head_norm:
        for c in range(acc.shape[1] // HEAD_DIM):
            cols = slice(c * HEAD_DIM, (c + 1) * HEAD_DIM)
            o_ref[:, cols] = _rms(acc[:, cols], hg_ref[...])
    else:
        o_ref[...] = acc


def _norm_matmul(x, g, w, *, n_out, col_off=0, head_g=None, tm=512, tn=1024):
    m, k = x.shape
    tm, tn = min(tm, m), min(tn, n_out)
    assert m % tm == 0 and n_out % tn == 0 and col_off % tn == 0
    off = col_off // tn
    in_specs = [
        pl.BlockSpec((tm, k), lambda i, j: (i, 0)),
        pl.BlockSpec((1, k), lambda i, j: (0, 0)),
        pl.BlockSpec((k, tn), lambda i, j: (0, j + off)),
    ]
    args = [x, g.reshape(1, k), w]
    if head_g is not None:
        in_specs.append(pl.BlockSpec((1, HEAD_DIM), lambda i, j: (0, 0)))
        args.append(head_g.reshape(1, HEAD_DIM))
    blocks = _nbytes((tm, k), F32) + _nbytes((k, tn), BF16) + _nbytes((tm, tn), F32)
    return pl.pallas_call(
        functools.partial(_norm_matmul_kernel, head_norm=head_g is not None),
        grid=(m // tm, n_out // tn),
        in_specs=in_specs,
        out_specs=pl.BlockSpec((tm, tn), lambda i, j: (i, j)),
        out_shape=jax.ShapeDtypeStruct((m, n_out), F32),
        scratch_shapes=[pltpu.VMEM((tm, k), BF16)],
        compiler_params=_params(("parallel", "arbitrary"), blocks, _nbytes((tm, k), BF16)),
        name="norm_matmul_headnorm" if head_g is not None else "norm_matmul",
    )(*args)


def _matmul_res_kernel(a_ref, w_ref, r_ref, o_ref):
    o_ref[...] = r_ref[...] + _dot(a_ref[...].astype(BF16), w_ref[...])


def _matmul_res(a, w, res, *, tm=512, tn=512):
    m, k = a.shape
    n = w.shape[1]
    tm, tn = min(tm, m), min(tn, n)
    assert m % tm == 0 and n % tn == 0
    blocks = _nbytes((tm, k), a.dtype) + _nbytes((k, tn), BF16) + 2 * _nbytes((tm, tn), F32)
    return pl.pallas_call(
        _matmul_res_kernel,
        grid=(m // tm, n // tn),
        in_specs=[
            pl.BlockSpec((tm, k), lambda i, j: (i, 0)),
            pl.BlockSpec((k, tn), lambda i, j: (0, j)),
            pl.BlockSpec((tm, tn), lambda i, j: (i, j)),
        ],
        out_specs=pl.BlockSpec((tm, tn), lambda i, j: (i, j)),
        out_shape=jax.ShapeDtypeStruct((m, n), F32),
        compiler_params=_params(("parallel", "arbitrary"), blocks),
        name="matmul_res",
    )(a, w, res)


def _ple_kernel(x_ref, xr_ref, p_ref, g_ref, wpe_ref, wg_ref, o_ref, xn_ref):
    @pl.when(pl.program_id(1) == 0)
    def _():
        xn_ref[...] = _rms(x_ref[...], g_ref[...]).astype(BF16)

    gate = jax.nn.sigmoid(_dot(xn_ref[...], wg_ref[...]))
    pe = _dot(p_ref[...].astype(BF16), wpe_ref[...])
    o_ref[...] = xr_ref[...] + pe * gate


def _ple(x, p, g, w_pe, w_gate, *, tm=512, tn=1024):
    m, d = x.shape
    pd = p.shape[1]
    tm, tn = min(tm, m), min(tn, d)
    assert m % tm == 0 and d % tn == 0
    blocks = (_nbytes((tm, d), F32) + 2 * _nbytes((tm, tn), F32) + _nbytes((tm, pd), F32)
              + _nbytes((pd, tn), BF16) + _nbytes((d, tn), BF16))
    return pl.pallas_call(
        _ple_kernel,
        grid=(m // tm, d // tn),
        in_specs=[
            pl.BlockSpec((tm, d), lambda i, j: (i, 0)),
            pl.BlockSpec((tm, tn), lambda i, j: (i, j)),
            pl.BlockSpec((tm, pd), lambda i, j: (i, 0)),
            pl.BlockSpec((1, d), lambda i, j: (0, 0)),
            pl.BlockSpec((pd, tn), lambda i, j: (0, j)),
            pl.BlockSpec((d, tn), lambda i, j: (0, j)),
        ],
        out_specs=pl.BlockSpec((tm, tn), lambda i, j: (i, j)),
        out_shape=jax.ShapeDtypeStruct((m, d), F32),
        scratch_shapes=[pltpu.VMEM((tm, d), BF16)],
        compiler_params=_params(("parallel", "arbitrary"), blocks, _nbytes((tm, d), BF16)),
        name="ple",
    )(x, x, p, g.reshape(1, d), w_pe, w_gate)


def _conv_gate_kernel(ug_ref, uu_ref, buf_ref, cw_ref, cb_ref, act_ref, st_ref):
    x = ug_ref[...]
    t_len = x.shape[1]
    t = lax.broadcasted_iota(jnp.int32, x.shape, 1)
    buf0 = buf_ref[:, 0:1, :]
    buf1 = buf_ref[:, 1:2, :]
    xm1 = jnp.where(t == 0, buf1, pltpu.roll(x, 1, axis=1))
    xm2 = jnp.where(t == 0, buf0, jnp.where(t == 1, buf1, pltpu.roll(x, 2, axis=1)))
    conv = cb_ref[...] + xm2 * cw_ref[0:1, :] + xm1 * cw_ref[1:2, :] + x * cw_ref[2:3, :]
    act_ref[...] = (_silu(conv) * uu_ref[...]).astype(act_ref.dtype)
    st_ref[...] = ug_ref[:, t_len - 2:t_len, :]


def _conv_gate(u, buf, conv_w, conv_b, *, bb, tk=512):
    b, t, f2 = u.shape
    f = f2 // 2
    assert b % bb == 0 and f % tk == 0 and t >= 2 and buf.shape[1] == 2
    nk = f // tk
    blocks = 2 * _nbytes((bb, t, tk), F32) + _nbytes((bb, t, tk), BF16) + 2 * _nbytes((bb, 2, tk), F32)
    return pl.pallas_call(
        _conv_gate_kernel,
        grid=(b // bb, nk),
        in_specs=[
            pl.BlockSpec((bb, t, tk), lambda i, j: (i, 0, j)),
            pl.BlockSpec((bb, t, tk), lambda i, j: (i, 0, j + nk)),
            pl.BlockSpec((bb, 2, tk), lambda i, j: (i, 0, j)),
            pl.BlockSpec((3, tk), lambda i, j: (0, j)),
            pl.BlockSpec((1, tk), lambda i, j: (0, j)),
        ],
        out_specs=[
            pl.BlockSpec((bb, t, tk), lambda i, j: (i, 0, j)),
            pl.BlockSpec((bb, 2, tk), lambda i, j: (i, 0, j)),
        ],
        out_shape=[
            jax.ShapeDtypeStruct((b, t, f), BF16),
            jax.ShapeDtypeStruct((b, 2, f), F32),
        ],
        compiler_params=_params(("parallel", "arbitrary"), blocks),
        name="conv_gate",
    )(u, u, buf, conv_w, conv_b.reshape(1, f))


def _cumsum_rows(x):
    n = x.shape[0]
    row = lax.broadcasted_iota(jnp.int32, x.shape, 0)
    s = 1
    while s < n:
        x = x + jnp.where(row >= s, pltpu.roll(x, s, axis=0), 0.0)
        s *= 2
    return x


def _hgrn_chunk(qz, fz, iv, state, lb, sub):
    c = qz.shape[0]
    f = lb + (1.0 - lb) * jax.nn.sigmoid(fz)
    q = _silu(qz)
    k = 1.0 - f
    b = _cumsum_rows(jnp.log(f))
    b_last = b[c - 1:c, :]
    o = _dot((q * jnp.exp(b)).astype(BF16), state.astype(BF16))

    lane = lax.broadcasted_iota(jnp.int32, (sub, c), 1)
    rowi = lax.broadcasted_iota(jnp.int32, (sub, c), 0)
    att_rows = []
    for i in range(c // sub):
        r0 = i * sub
        bi, qi, ki = b[r0:r0 + sub], q[r0:r0 + sub], k[r0:r0 + sub]
        att = jnp.zeros((sub, c), F32)
        for s in range(sub):
            pair = qi * ki[s:s + 1] * jnp.exp(jnp.minimum(bi - bi[s:s + 1], 0.0))
            col = jnp.sum(pair, axis=1, keepdims=True)
            att = jnp.where((lane == r0 + s) & (rowi >= s), col, att)
        if i > 0:
            mid = bi[0:1]
            qt = qi * jnp.exp(bi - mid)
            kt = k[:r0] * jnp.exp(mid - b[:r0])
            kt = jnp.concatenate([kt, jnp.zeros((c - r0, HEAD_DIM), F32)], axis=0)
            att = att + _dot(qt.astype(BF16), kt.astype(BF16), NT_DIMS)
        att_rows.append(att)
    att = att_rows[0] if len(att_rows) == 1 else jnp.concatenate(att_rows, axis=0)
    o = o + _dot(att.astype(BF16), iv.astype(BF16))

    decay = jnp.transpose(jnp.broadcast_to(jnp.exp(b_last), (HEAD_DIM, HEAD_DIM)))
    k_st = k * jnp.exp(b_last - b)
    new_state = decay * state + _dot(k_st.astype(BF16), iv.astype(BF16), TN_DIMS)
    return o, new_state


def _hgrn_kernel(q_ref, f_ref, i_ref, g_ref, alb_ref, gn_ref, s0_ref, og_ref, sout_ref, s_ref,
                 *, layer, chunk, sub):
    t_step = pl.program_id(2)

    @pl.when(t_step == 0)
    def _():
        s_ref[...] = s0_ref[...]

    a = alb_ref[...]
    e = jnp.exp(a - jnp.max(a, axis=0, keepdims=True))
    sm = e / jnp.sum(e, axis=0, keepdims=True)
    lb = jnp.sum(sm[:layer + 1], axis=0, keepdims=True)

    bb, tc, _ = q_ref.shape

    def batch_body(bi, carry):
        def chunk_body(ci, state):
            rows = pl.ds(pl.multiple_of(ci * chunk, chunk), chunk)
            o, state = _hgrn_chunk(q_ref[bi, rows, :], f_ref[bi, rows, :], i_ref[bi, rows, :],
                                   state, lb, sub)
            og_ref[bi, rows, :] = _rms(o, gn_ref[...]) * _silu(g_ref[bi, rows, :])
            return state

        s_ref[bi] = lax.fori_loop(0, tc // chunk, chunk_body, s_ref[bi])
        return carry

    lax.fori_loop(0, bb, batch_body, 0)

    @pl.when(t_step == pl.num_programs(2) - 1)
    def _():
        sout_ref[...] = s_ref[...]


def _hgrn(proj, a_lb, g_norm, s0, *, layer, bb, tc):
    b, t, d4 = proj.shape
    d = d4 // 4
    h = d // HEAD_DIM
    chunk = min(HGRN_CHUNK, t)
    sub = min(HGRN_SUB, chunk)
    assert b % bb == 0 and t % tc == 0 and tc % chunk == 0 and chunk % sub == 0
    nl = a_lb.shape[0]

    def col(off):
        return pl.BlockSpec((bb, tc, HEAD_DIM), lambda bi, hi, ti: (bi, ti, hi + off * h))

    state_spec = pl.BlockSpec((bb, None, HEAD_DIM, HEAD_DIM), lambda bi, hi, ti: (bi, hi, 0, 0))
    blocks = 5 * _nbytes((bb, tc, HEAD_DIM), F32) + 2 * _nbytes((bb, HEAD_DIM, HEAD_DIM), F32)
    return pl.pallas_call(
        functools.partial(_hgrn_kernel, layer=layer, chunk=chunk, sub=sub),
        grid=(b // bb, h, t // tc),
        in_specs=[
            col(0), col(1), col(2), col(3),
            pl.BlockSpec((nl, HEAD_DIM), lambda bi, hi, ti: (0, hi)),
            pl.BlockSpec((1, HEAD_DIM), lambda bi, hi, ti: (0, 0)),
            state_spec,
        ],
        out_specs=[
            pl.BlockSpec((bb, tc, HEAD_DIM), lambda bi, hi, ti: (bi, ti, hi)),
            state_spec,
        ],
        out_shape=[
            jax.ShapeDtypeStruct((b, t, d), F32),
            jax.ShapeDtypeStruct(s0.shape, F32),
        ],
        scratch_shapes=[pltpu.VMEM((bb, HEAD_DIM, HEAD_DIM), F32)],
        compiler_params=_params(("parallel", "parallel", "arbitrary"), blocks,
                                _nbytes((bb, HEAD_DIM, HEAD_DIM), F32)),
        name="hgrn",
    )(proj, proj, proj, proj, a_lb, g_norm.reshape(1, HEAD_DIM), s0)


def _sb_prompt_kernel(bias_ref, q_ref, k_ref, v_ref, o_ref, *, blk, scale):
    head = pl.program_id(1)
    qi = pl.program_id(2)
    bias = bias_ref[head]
    q = (q_ref[...] * scale).astype(BF16)
    row = lax.broadcasted_iota(jnp.int32, (blk, blk), 0)
    colm = lax.broadcasted_iota(jnp.int32, (blk, blk), 1)
    suffix = (row >= colm).astype(BF16)

    def visit(kb, acc, r, mask):
        rows = pl.ds(pl.multiple_of(kb * blk, blk), blk)
        k = k_ref[rows, :].astype(BF16)
        v = v_ref[rows, :].astype(BF16)
        z = _dot(q, k, NT_DIMS) + bias
        sp = _softplus(z)
        log_rest = -sp
        if mask is not None:
            log_rest = jnp.where(mask, log_rest, 0.0)
        hi, lo = _bf16_split(log_rest)
        csum = _dot(hi, suffix) + _dot(lo, suffix)
        a = jnp.exp((z - sp) + (csum - log_rest) + r)
        if mask is not None:
            a = jnp.where(mask, a, 0.0)
        return acc + _dot(a.astype(BF16), v), r + csum[:, 0:1]

    acc = jnp.zeros((blk, HEAD_DIM), F32)
    r = jnp.zeros((blk, 1), F32)
    acc, r = visit(qi, acc, r, colm < row)

    def older(step, carry):
        return visit(qi - 1 - step, carry[0], carry[1], None)

    acc, r = lax.fori_loop(0, qi, older, (acc, r))
    o_ref[...] = acc.astype(o_ref.dtype)


def _sb_prompt(q, k, v, bias):
    b, t, d = q.shape
    h = d // HEAD_DIM
    blk = min(SB_BLOCK, t)
    assert t % blk == 0
    kv_spec = pl.BlockSpec((None, t, HEAD_DIM), lambda bi, hi, qi: (bi, 0, hi))
    q_spec = pl.BlockSpec((None, blk, HEAD_DIM), lambda bi, hi, qi: (bi, qi, hi))
    blocks = 2 * _nbytes((t, HEAD_DIM), F32) + _nbytes((blk, HEAD_DIM), F32) + _nbytes((blk, HEAD_DIM), BF16)
    return pl.pallas_call(
        functools.partial(_sb_prompt_kernel, blk=blk, scale=HEAD_DIM ** -0.5),
        grid=(b, h, t // blk),
        in_specs=[pl.BlockSpec(memory_space=pltpu.SMEM), q_spec, kv_spec, kv_spec],
        out_specs=q_spec,
        out_shape=jax.ShapeDtypeStruct((b, t, d), BF16),
        compiler_params=_params(("parallel", "parallel", "arbitrary"), blocks),
        name="sb_prompt",
    )(bias, q, k, v)


def _sb_sample_kernel(pt_ref, q_ref, kn_ref, vn_ref, bias_ref, *rest, pages, scale):
    k_refs, v_refs = rest[:pages], rest[pages:2 * pages]
    o_ref, qbd_ref, kc_ref, vc_ref, acc_ref, r_ref = rest[2 * pages:]
    step = pl.program_id(1)
    nq, d = q_ref.shape
    h = d // HEAD_DIM
    page = k_refs[0].shape[0]
    lanes = h * nq

    def visit(n, mask):
        kc = kc_ref[0:n, :]
        z = _dot(kc, qbd_ref[...], NT_DIMS) + bias_ref[...]
        sp = _softplus(z)
        log_rest = -sp
        if mask is not None:
            log_rest = jnp.where(mask, log_rest, 0.0)
        ri = lax.broadcasted_iota(jnp.int32, (n, n), 0)
        ci = lax.broadcasted_iota(jnp.int32, (n, n), 1)
        newer = (ci > ri).astype(BF16)
        hi, lo = _bf16_split(log_rest)
        csum = _dot(newer, hi) + _dot(newer, lo)
        a = jnp.exp((z - sp) + csum + r_ref[...])
        if mask is not None:
            a = jnp.where(mask, a, 0.0)
        acc_ref[...] += _dot(jnp.transpose(a).astype(BF16), vc_ref[0:n, :])
        r_ref[...] += jnp.sum(log_rest, axis=0, keepdims=True)

    @pl.when(step == 0)
    def _():
        qs = q_ref[...] * scale
        qt = jnp.broadcast_to(qs[None], (h, nq, d)).reshape(lanes, d)
        rr = lax.broadcasted_iota(jnp.int32, (lanes, d), 0)
        cc = lax.broadcasted_iota(jnp.int32, (lanes, d), 1)
        qbd_ref[...] = jnp.where(rr // nq == cc // HEAD_DIM, qt, 0.0).astype(BF16)
        acc_ref[...] = jnp.zeros_like(acc_ref)
        r_ref[...] = jnp.zeros_like(r_ref)
        pad = jnp.zeros((page - nq, d), BF16)
        kc_ref[0:page, :] = jnp.concatenate([kn_ref[...].astype(BF16), pad], axis=0)
        vc_ref[0:page, :] = jnp.concatenate([vn_ref[...].astype(BF16), pad], axis=0)
        key = lax.broadcasted_iota(jnp.int32, (page, lanes), 0)
        query = lax.broadcasted_iota(jnp.int32, (page, lanes), 1) % nq
        visit(page, key < query)

    @pl.when(step > 0)
    def _():
        for p in range(pages):
            rows = slice((pages - 1 - p) * page, (pages - p) * page)
            kc_ref[rows, :] = k_refs[p][...].astype(BF16)
            vc_ref[rows, :] = v_refs[p][...].astype(BF16)
        visit(pages * page, None)

    @pl.when(step == pl.num_programs(1) - 1)
    def _():
        for hh in range(h):
            cols = slice(hh * HEAD_DIM, (hh + 1) * HEAD_DIM)
            o_ref[:, cols] = acc_ref[hh * nq:(hh + 1) * nq, cols]


def _sb_sample(q, k_new, v_new, bias, cache_k, cache_v, page_table):
    b, nq, d = q.shape
    h = d // HEAD_DIM
    n_pool, page = cache_k.shape[:2]
    n_pages = page_table.shape[1]
    pages = min(SB_PAGES_PER_STEP, n_pages)
    assert n_pages % pages == 0 and h * nq == HEAD_DIM and nq <= page
    ck = cache_k.reshape(n_pool, page, d)
    cv = cache_v.reshape(n_pool, page, d)
    lanes = h * nq

    def page_spec(p):
        def index(bi, si, pt):
            group = jnp.maximum(si - 1, 0)
            return (pt[bi * n_pages + n_pages - 1 - (group * pages + p)], 0, 0)
        return pl.BlockSpec((None, page, d), index)

    tok_spec = pl.BlockSpec((None, nq, d), lambda bi, si, pt: (bi, 0, 0))
    scratch = [
        pltpu.VMEM((lanes, d), BF16),
        pltpu.VMEM((pages * page, d), BF16),
        pltpu.VMEM((pages * page, d), BF16),
        pltpu.VMEM((lanes, d), F32),
        pltpu.VMEM((1, lanes), F32),
    ]
    blocks = 2 * pages * _nbytes((page, d), F32) + 4 * _nbytes((nq, d), F32)
    scratch_bytes = (_nbytes((lanes, d), BF16) + 2 * _nbytes((pages * page, d), BF16)
                     + _nbytes((lanes, d), F32))
    grid_spec = pltpu.PrefetchScalarGridSpec(
        num_scalar_prefetch=1,
        grid=(b, 1 + n_pages // pages),
        in_specs=[tok_spec, tok_spec, tok_spec,
                  pl.BlockSpec((1, lanes), lambda bi, si, pt: (0, 0))]
                 + [page_spec(p) for p in range(pages)] * 2,
        out_specs=tok_spec,
        scratch_shapes=scratch,
    )
    return pl.pallas_call(
        functools.partial(_sb_sample_kernel, pages=pages, scale=HEAD_DIM ** -0.5),
        grid_spec=grid_spec,
        out_shape=jax.ShapeDtypeStruct((b, nq, d), F32),
        compiler_params=_params(("parallel", "arbitrary"), blocks, scratch_bytes),
        name="sb_sample",
    )(page_table.reshape(-1), q, k_new, v_new, jnp.repeat(bias, nq).reshape(1, lanes),
      *([ck] * pages), *([cv] * pages))


def _run_trunk(x3, p, hgrn_state, conv_state, attend, w, *, hgrn_bb, hgrn_tc, conv_bb):
    b, t, d = x3.shape
    m = b * t
    depth = w["norm_mix"].shape[0]
    n_a = w["a_w_in"].shape[0]
    f = w["ffn_w_out"].shape[1]
    x = x3.reshape(m, d)
    new_s, new_c = [], []
    k_sh = v_sh = None
    for layer in range(depth):
        if layer < n_a:
            proj = _norm_matmul(x, w["norm_mix"][layer], w["a_w_in"][layer], n_out=4 * d)
            og, s = _hgrn(proj.reshape(b, t, 4 * d), w["a_lb"], w["a_g_norm"][layer], hgrn_state[layer],
                          layer=layer, bb=hgrn_bb, tc=hgrn_tc)
            new_s.append(s)
            x = _matmul_res(og.reshape(m, d), w["a_w_out"][layer], x, tn=1024)
        else:
            bi = layer - n_a
            q = _norm_matmul(x, w["norm_mix"][layer], w["b_w_q"][bi], n_out=d, head_g=w["b_q_norm"][bi])
            o = attend(q.reshape(b, t, d), k_sh, v_sh, w["b_sb_bias"][bi])
            x = _matmul_res(o.reshape(m, d), w["b_w_out"][bi], x, tn=1024)
        u = _norm_matmul(x, w["norm_ffn"][layer], w["ffn_w_in"][layer], n_out=2 * f)
        act, cbuf = _conv_gate(u.reshape(b, t, 2 * f), conv_state[layer], w["ffn_conv_w"][layer],
                               w["ffn_conv_b"][layer], bb=conv_bb)
        new_c.append(cbuf)
        x = _matmul_res(act.reshape(m, f), w["ffn_w_out"][layer], x)
        x = _ple(x, p[layer].reshape(m, -1), w["ple_norm"][layer], w["ple_w"][layer], w["ple_w_gate"][layer])
        if layer == n_a - 1:
            k_sh = _norm_matmul(x, w["kv_norm"], w["kv_w"], n_out=d, head_g=w["k_norm"]).reshape(b, t, d)
            v_sh = _norm_matmul(x, w["kv_norm"], w["kv_w"], n_out=d, col_off=d).reshape(b, t, d)
    return x.reshape(b, t, d), k_sh, v_sh, jnp.stack(new_s), jnp.stack(new_c)


def kernel(x_prompt, x_sample, p_prompt, p_sample, cache_k, cache_v, page_table, state_hgrn, state_conv, norm_mix, norm_ffn, a_w_in, a_lb, a_g_norm, a_w_out, kv_norm, kv_w, k_norm, b_w_q, b_q_norm, b_sb_bias, b_w_out, ffn_w_in, ffn_conv_w, ffn_conv_b, ffn_w_out, ple_w, ple_norm, ple_w_gate):
    w = dict(norm_mix=norm_mix, norm_ffn=norm_ffn, a_lb=a_lb, a_g_norm=a_g_norm, kv_norm=kv_norm,
             k_norm=k_norm, b_q_norm=b_q_norm, b_sb_bias=b_sb_bias, ffn_conv_w=ffn_conv_w,
             ffn_conv_b=ffn_conv_b, ple_norm=ple_norm)
    for name, val in dict(a_w_in=a_w_in, a_w_out=a_w_out, kv_w=kv_w, b_w_q=b_w_q, b_w_out=b_w_out,
                          ffn_w_in=ffn_w_in, ffn_w_out=ffn_w_out, ple_w=ple_w, ple_w_gate=ple_w_gate).items():
        w[name] = val.astype(BF16)

    b, t, d = x_prompt.shape
    heads = d // HEAD_DIM
    n_a = a_w_in.shape[0]
    depth = norm_mix.shape[0]
    f = ffn_w_out.shape[1]
    zero_h = jnp.zeros((n_a, b, heads, HEAD_DIM, HEAD_DIM), state_hgrn.dtype)
    zero_c = jnp.zeros((depth, b, state_conv.shape[2], f), state_conv.dtype)

    def attend_prompt(q, k, v, bias):
        return _sb_prompt(q, k, v, bias)

    def attend_sample(q, k, v, bias):
        return _sb_sample(q, k, v, bias, cache_k, cache_v, page_table)

    y_p, k_p, v_p, h_p, c_p = _run_trunk(x_prompt, p_prompt, zero_h, zero_c, attend_prompt, w,
                                         hgrn_bb=1, hgrn_tc=512, conv_bb=1)
    y_s, k_s, v_s, h_s, c_s = _run_trunk(x_sample, p_sample, state_hgrn, state_conv, attend_sample, w,
                                         hgrn_bb=8, hgrn_tc=x_sample.shape[1], conv_bb=x_sample.shape[0])
    hs = (heads, HEAD_DIM)
    return (y_p, y_s,
            k_p.reshape(k_p.shape[:2] + hs), v_p.reshape(v_p.shape[:2] + hs),
            k_s.reshape(k_s.shape[:2] + hs), v_s.reshape(v_s.shape[:2] + hs),
            h_p, h_s, c_p, c_s)
```

```python
import functools

import jax
import jax.numpy as jnp
from jax import lax
from jax.experimental import pallas as pl
from jax.experimental.pallas import tpu as pltpu

F32 = jnp.float32
BF16 = jnp.bfloat16
EPS = 1e-6
HEAD_DIM = 128
SUBLANES = 8
HGRN_CHUNK = 64
HGRN_SUB = 16
SB_BLOCK_Q = 512
SB_BLOCK_K = 512
SB_PAGES_PER_STEP = 4
SB_HEADS_PER_STEP = 1
V7X_VMEM_CAP = 56 * 1024 * 1024
VMEM_TEMP_MARGIN = 16 * 1024 * 1024

NT_DIMS = (((1,), (1,)), ((), ()))
TN_DIMS = (((0,), (0,)), ((), ()))


def _params(semantics, block_bytes, scratch_bytes=0):
    limit = min(2 * block_bytes + scratch_bytes + VMEM_TEMP_MARGIN, V7X_VMEM_CAP)
    return pltpu.CompilerParams(dimension_semantics=semantics, vmem_limit_bytes=limit)


def _nbytes(shape, dtype):
    n = jnp.dtype(dtype).itemsize
    for s in shape:
        n *= s
    return n


def _rms(x, g):
    return x * lax.rsqrt(jnp.mean(x * x, axis=-1, keepdims=True) + EPS) * g


def _silu(x):
    return x * jax.nn.sigmoid(x)


def _softplus(z):
    return jnp.maximum(z, 0.0) + jnp.log(1.0 + jnp.exp(-jnp.abs(z)))


def _bf16_split(x):
    hi = x.astype(BF16)
    lo = (x - hi.astype(F32)).astype(BF16)
    return hi, lo


def _dot(a, b, dims=None):
    if dims is None:
        return jnp.dot(a, b, preferred_element_type=F32)
    return lax.dot_general(a, b, dims, preferred_element_type=F32)


def _norm_matmul_kernel(x_ref, g_ref, w_ref, *rest, head_norm):
    if head_norm:
        hg_ref, o_ref, xn_ref = rest
    else:
        o_ref, xn_ref = rest

    @pl.when(pl.program_id(1) == 0)
    def _():
        xn_ref[...] = _rms(x_ref[...], g_ref[...]).astype(BF16)

    acc = _dot(xn_ref[...], w_ref[...])
    if head_norm:
        for c in range(acc.shape[1] // HEAD_DIM):
            cols = slice(c * HEAD_DIM, (c + 1) * HEAD_DIM)
            o_ref[:, cols] = _rms(acc[:, cols], hg_ref[...])
    else:
        o_ref[...] = acc


def _norm_matmul(x, g, w, *, n_out, col_off=0, head_g=None, tm=1024, tn=1024):
    m, k = x.shape
    tm, tn = min(tm, m), min(tn, n_out)
    assert m % tm == 0 and n_out % tn == 0 and col_off % tn == 0
    off = col_off // tn
    in_specs = [
        pl.BlockSpec((tm, k), lambda i, j: (i, 0)),
        pl.BlockSpec((1, k), lambda i, j: (0, 0)),
        pl.BlockSpec((k, tn), lambda i, j: (0, j + off)),
    ]
    args = [x, g.reshape(1, k), w]
    if head_g is not None:
        in_specs.append(pl.BlockSpec((1, HEAD_DIM), lambda i, j: (0, 0)))
        args.append(head_g.reshape(1, HEAD_DIM))
    blocks = _nbytes((tm, k), F32) + _nbytes((k, tn), BF16) + _nbytes((tm, tn), F32)
    return pl.pallas_call(
        functools.partial(_norm_matmul_kernel, head_norm=head_g is not None),
        grid=(m // tm, n_out // tn),
        in_specs=in_specs,
        out_specs=pl.BlockSpec((tm, tn), lambda i, j: (i, j)),
        out_shape=jax.ShapeDtypeStruct((m, n_out), F32),
        scratch_shapes=[pltpu.VMEM((tm, k), BF16)],
        compiler_params=_params(("parallel", "arbitrary"), blocks, _nbytes((tm, k), BF16)),
        name="norm_matmul_headnorm" if head_g is not None else "norm_matmul",
    )(*args)


def _matmul_res_kernel(a_ref, w_ref, r_ref, o_ref):
    o_ref[...] = r_ref[...] + _dot(a_ref[...].astype(BF16), w_ref[...])


def _matmul_res(a, w, res, *, tm=1024, tn=512):
    m, k = a.shape
    n = w.shape[1]
    tm, tn = min(tm, m), min(tn, n)
    assert m % tm == 0 and n % tn == 0
    blocks = _nbytes((tm, k), a.dtype) + _nbytes((k, tn), BF16) + 2 * _nbytes((tm, tn), F32)
    return pl.pallas_call(
        _matmul_res_kernel,
        grid=(m // tm, n // tn),
        in_specs=[
            pl.BlockSpec((tm, k), lambda i, j: (i, 0)),
            pl.BlockSpec((k, tn), lambda i, j: (0, j)),
            pl.BlockSpec((tm, tn), lambda i, j: (i, j)),
        ],
        out_specs=pl.BlockSpec((tm, tn), lambda i, j: (i, j)),
        out_shape=jax.ShapeDtypeStruct((m, n), F32),
        compiler_params=_params(("parallel", "arbitrary"), blocks),
        name="matmul_res",
    )(a, w, res)


def _ple_kernel(x_ref, xr_ref, p_ref, g_ref, wpe_ref, wg_ref, o_ref, xn_ref):
    @pl.when(pl.program_id(1) == 0)
    def _():
        xn_ref[...] = _rms(x_ref[...], g_ref[...]).astype(BF16)

    gate = jax.nn.sigmoid(_dot(xn_ref[...], wg_ref[...]))
    pe = _dot(p_ref[...].astype(BF16), wpe_ref[...])
    o_ref[...] = xr_ref[...] + pe * gate


def _ple(x, p, g, w_pe, w_gate, *, tm=512, tn=1024):
    m, d = x.shape
    pd = p.shape[1]
    tm, tn = min(tm, m), min(tn, d)
    assert m % tm == 0 and d % tn == 0
    blocks = (_nbytes((tm, d), F32) + 2 * _nbytes((tm, tn), F32) + _nbytes((tm, pd), F32)
              + _nbytes((pd, tn), BF16) + _nbytes((d, tn), BF16))
    return pl.pallas_call(
        _ple_kernel,
        grid=(m // tm, d // tn),
        in_specs=[
            pl.BlockSpec((tm, d), lambda i, j: (i, 0)),
            pl.BlockSpec((tm, tn), lambda i, j: (i, j)),
            pl.BlockSpec((tm, pd), lambda i, j: (i, 0)),
            pl.BlockSpec((1, d), lambda i, j: (0, 0)),
            pl.BlockSpec((pd, tn), lambda i, j: (0, j)),
            pl.BlockSpec((d, tn), lambda i, j: (0, j)),
        ],
        out_specs=pl.BlockSpec((tm, tn), lambda i, j: (i, j)),
        out_shape=jax.ShapeDtypeStruct((m, d), F32),
        scratch_shapes=[pltpu.VMEM((tm, d), BF16)],
        compiler_params=_params(("parallel", "arbitrary"), blocks, _nbytes((tm, d), BF16)),
        name="ple",
    )(x, x, p, g.reshape(1, d), w_pe, w_gate)


def _conv_gate_kernel(ug_ref, uu_ref, buf_ref, cw_ref, cb_ref, act_ref, st_ref):
    x = ug_ref[...]
    t_len = x.shape[1]
    t = lax.broadcasted_iota(jnp.int32, x.shape, 1)
    buf0 = buf_ref[:, 0:1, :]
    buf1 = buf_ref[:, 1:2, :]
    xm1 = jnp.where(t == 0, buf1, pltpu.roll(x, 1, axis=1))
    xm2 = jnp.where(t == 0, buf0, jnp.where(t == 1, buf1, pltpu.roll(x, 2, axis=1)))
    conv = cb_ref[...] + xm2 * cw_ref[0:1, :] + xm1 * cw_ref[1:2, :] + x * cw_ref[2:3, :]
    act_ref[...] = (_silu(conv) * uu_ref[...]).astype(act_ref.dtype)
    st_ref[...] = ug_ref[:, t_len - 2:t_len, :]


def _conv_gate(u, buf, conv_w, conv_b, *, bb, tk=512):
    b, t, f2 = u.shape
    f = f2 // 2
    assert b % bb == 0 and f % tk == 0 and t >= 2 and buf.shape[1] == 2
    nk = f // tk
    blocks = 2 * _nbytes((bb, t, tk), F32) + _nbytes((bb, t, tk), BF16) + 2 * _nbytes((bb, 2, tk), F32)
    return pl.pallas_call(
        _conv_gate_kernel,
        grid=(b // bb, nk),
        in_specs=[
            pl.BlockSpec((bb, t, tk), lambda i, j: (i, 0, j)),
            pl.BlockSpec((bb, t, tk), lambda i, j: (i, 0, j + nk)),
            pl.BlockSpec((bb, 2, tk), lambda i, j: (i, 0, j)),
            pl.BlockSpec((3, tk), lambda i, j: (0, j)),
            pl.BlockSpec((1, tk), lambda i, j: (0, j)),
        ],
        out_specs=[
            pl.BlockSpec((bb, t, tk), lambda i, j: (i, 0, j)),
            pl.BlockSpec((bb, 2, tk), lambda i, j: (i, 0, j)),
        ],
        out_shape=[
            jax.ShapeDtypeStruct((b, t, f), BF16),
            jax.ShapeDtypeStruct((b, 2, f), F32),
        ],
        compiler_params=_params(("parallel", "arbitrary"), blocks),
        name="conv_gate",
    )(u, u, buf, conv_w, conv_b.reshape(1, f))


def _cumsum_rows(x):
    n = x.shape[0]
    row = lax.broadcasted_iota(jnp.int32, x.shape, 0)
    s = 1
    while s < n:
        x = x + jnp.where(row >= s, pltpu.roll(x, s, axis=0), 0.0)
        s *= 2
    return x


def _hgrn_chunk(qz, fz, iv, state, lb, sub):
    c = qz.shape[0]
    f = lb + (1.0 - lb) * jax.nn.sigmoid(fz)
    q = _silu(qz)
    k = 1.0 - f
    b = _cumsum_rows(jnp.log(f))
    b_last = b[c - 1:c, :]
    o = _dot((q * jnp.exp(b)).astype(BF16), state.astype(BF16))

    lane = lax.broadcasted_iota(jnp.int32, (sub, c), 1)
    rowi = lax.broadcasted_iota(jnp.int32, (sub, c), 0)
    att_rows = []
    for i in range(c // sub):
        r0 = i * sub
        bi, qi, ki = b[r0:r0 + sub], q[r0:r0 + sub], k[r0:r0 + sub]
        att = jnp.zeros((sub, c), F32)
        for s in range(sub):
            pair = qi * ki[s:s + 1] * jnp.exp(jnp.minimum(bi - bi[s:s + 1], 0.0))
            col = jnp.sum(pair, axis=1, keepdims=True)
            att = jnp.where((lane == r0 + s) & (rowi >= s), col, att)
        if i > 0:
            mid = bi[0:1]
            qt = qi * jnp.exp(bi - mid)
            kt = k[:r0] * jnp.exp(mid - b[:r0])
            kt = jnp.concatenate([kt, jnp.zeros((c - r0, HEAD_DIM), F32)], axis=0)
            att = att + _dot(qt.astype(BF16), kt.astype(BF16), NT_DIMS)
        att_rows.append(att)
    att = att_rows[0] if len(att_rows) == 1 else jnp.concatenate(att_rows, axis=0)
    o = o + _dot(att.astype(BF16), iv.astype(BF16))

    decay = jnp.transpose(jnp.broadcast_to(jnp.exp(b_last), (HEAD_DIM, HEAD_DIM)))
    k_st = k * jnp.exp(b_last - b)
    new_state = decay * state + _dot(k_st.astype(BF16), iv.astype(BF16), TN_DIMS)
    return o, new_state


def _hgrn_kernel(q_ref, f_ref, i_ref, g_ref, alb_ref, gn_ref, s0_ref, og_ref, sout_ref, s_ref,
                 *, layer, chunk, sub):
    t_step = pl.program_id(2)

    @pl.when(t_step == 0)
    def _():
        s_ref[...] = s0_ref[...]

    a = alb_ref[...]
    e = jnp.exp(a - jnp.max(a, axis=0, keepdims=True))
    sm = e / jnp.sum(e, axis=0, keepdims=True)
    lb = jnp.sum(sm[:layer + 1], axis=0, keepdims=True)

    bb, tc, width = q_ref.shape

    def chunk_body(ci, carry):
        rows = pl.ds(pl.multiple_of(ci * chunk, chunk), chunk)
        for bi in range(bb):
            for hh in range(width // HEAD_DIM):
                cols = slice(hh * HEAD_DIM, (hh + 1) * HEAD_DIM)
                o, state = _hgrn_chunk(q_ref[bi, rows, cols], f_ref[bi, rows, cols], i_ref[bi, rows, cols],
                                       s_ref[bi, hh], lb[:, cols], sub)
                og_ref[bi, rows, cols] = _rms(o, gn_ref[...]) * _silu(g_ref[bi, rows, cols])
                s_ref[bi, hh] = state
        return carry

    lax.fori_loop(0, tc // chunk, chunk_body, 0)

    @pl.when(t_step == pl.num_programs(2) - 1)
    def _():
        sout_ref[...] = s_ref[...]


def _hgrn(proj, a_lb, g_norm, s0, *, layer, bb, hp, tc):
    b, t, d4 = proj.shape
    d = d4 // 4
    h = d // HEAD_DIM
    chunk = min(HGRN_CHUNK, t)
    sub = min(HGRN_SUB, chunk)
    assert b % bb == 0 and h % hp == 0 and t % tc == 0 and tc % chunk == 0 and chunk % sub == 0
    nl = a_lb.shape[0]
    width = hp * HEAD_DIM
    hg = h // hp

    def col(off):
        return pl.BlockSpec((bb, tc, width), lambda bi, hi, ti: (bi, ti, hi + off * hg))

    state_shape = (bb, hp, HEAD_DIM, HEAD_DIM)
    state_spec = pl.BlockSpec(state_shape, lambda bi, hi, ti: (bi, hi, 0, 0))
    blocks = 5 * _nbytes((bb, tc, width), F32) + 2 * _nbytes(state_shape, F32)
    return pl.pallas_call(
        functools.partial(_hgrn_kernel, layer=layer, chunk=chunk, sub=sub),
        grid=(b // bb, hg, t // tc),
        in_specs=[
            col(0), col(1), col(2), col(3),
            pl.BlockSpec((nl, width), lambda bi, hi, ti: (0, hi)),
            pl.BlockSpec((1, HEAD_DIM), lambda bi, hi, ti: (0, 0)),
            state_spec,
        ],
        out_specs=[
            pl.BlockSpec((bb, tc, width), lambda bi, hi, ti: (bi, ti, hi)),
            state_spec,
        ],
        out_shape=[
            jax.ShapeDtypeStruct((b, t, d), F32),
            jax.ShapeDtypeStruct(s0.shape, F32),
        ],
        scratch_shapes=[pltpu.VMEM(state_shape, F32)],
        compiler_params=_params(("parallel", "parallel", "arbitrary"), blocks, _nbytes(state_shape, F32)),
        name="hgrn",
    )(proj, proj, proj, proj, a_lb, g_norm.reshape(1, HEAD_DIM), s0)


def _sb_prompt_kernel(bias_ref, q_ref, k_ref, v_ref, o_ref, *, bq, bk, scale):
    group = pl.program_id(1)
    qi = pl.program_id(2)
    hp = q_ref.shape[1] // HEAD_DIM
    nsub = bq // bk
    heads = [slice(hh * HEAD_DIM, (hh + 1) * HEAD_DIM) for hh in range(hp)]
    bias = [bias_ref[group * hp + hh] for hh in range(hp)]
    q = [(q_ref[:, cols] * scale).astype(BF16) for cols in heads]
    row = lax.broadcasted_iota(jnp.int32, (bk, bk), 0)
    colm = lax.broadcasted_iota(jnp.int32, (bk, bk), 1)
    suffix = (row >= colm).astype(BF16)
    q_pos = lax.broadcasted_iota(jnp.int32, (bq, bk), 0)
    k_pos = lax.broadcasted_iota(jnp.int32, (bq, bk), 1)

    def visit(kb, carry, mask):
        rows = pl.ds(pl.multiple_of(kb * bk, bk), bk)
        out = []
        for hh, cols in enumerate(heads):
            acc, r = carry[hh]
            k = k_ref[rows, cols].astype(BF16)
            v = v_ref[rows, cols].astype(BF16)
            z = _dot(q[hh], k, NT_DIMS) + bias[hh]
            log_rest = -_softplus(z)
            if mask is not None:
                log_rest = jnp.where(mask, log_rest, 0.0)
            hi, lo = _bf16_split(log_rest)
            csum = _dot(hi, suffix) + _dot(lo, suffix)
            a = jnp.exp(z + csum + r)
            if mask is not None:
                a = jnp.where(mask, a, 0.0)
            out.append((acc + _dot(a.astype(BF16), v), r + csum[:, 0:1]))
        return tuple(out)

    carry = ((jnp.zeros((bq, HEAD_DIM), F32), jnp.zeros((bq, 1), F32)),) * hp
    for sub in reversed(range(nsub)):
        carry = visit(qi * nsub + sub, carry, k_pos + sub * bk < q_pos)
    carry = lax.fori_loop(0, qi * nsub, lambda step, c: visit(qi * nsub - 1 - step, c, None), carry)
    for hh, cols in enumerate(heads):
        o_ref[:, cols] = carry[hh][0].astype(o_ref.dtype)


def _sb_prompt(q, k, v, bias):
    b, t, d = q.shape
    bq, bk = min(SB_BLOCK_Q, t), min(SB_BLOCK_K, t)
    width = SB_HEADS_PER_STEP * HEAD_DIM
    assert t % bq == 0 and bq % bk == 0 and d % width == 0
    kv_spec = pl.BlockSpec((None, t, width), lambda bi, hi, qi: (bi, 0, hi))
    q_spec = pl.BlockSpec((None, bq, width), lambda bi, hi, qi: (bi, qi, hi))
    blocks = 2 * _nbytes((t, width), F32) + _nbytes((bq, width), F32) + _nbytes((bq, width), BF16)
    return pl.pallas_call(
        functools.partial(_sb_prompt_kernel, bq=bq, bk=bk, scale=HEAD_DIM ** -0.5),
        grid=(b, d // width, t // bq),
        in_specs=[pl.BlockSpec(memory_space=pltpu.SMEM), q_spec, kv_spec, kv_spec],
        out_specs=q_spec,
        out_shape=jax.ShapeDtypeStruct((b, t, d), BF16),
        compiler_params=_params(("parallel", "parallel", "arbitrary"), blocks),
        name="sb_prompt",
    )(bias, q, k, v)


def _slab_head(slab_ref, hh):
    keys, nh, dim = slab_ref.shape
    return slab_ref.reshape(keys * nh, dim)[pl.ds(hh, keys, stride=nh), :]


def _sb_sample_kernel(pt_ref, q_ref, kn_ref, vn_ref, bias_ref, *rest, pages, scale):
    nq, d = q_ref.shape
    h = d // HEAD_DIM
    groups = h // SUBLANES
    n_slabs = pages * groups
    k_refs, v_refs = rest[:n_slabs], rest[n_slabs:2 * n_slabs]
    o_ref, qbd_ref, kc_ref, vc_ref, acc_ref, r_ref = rest[2 * n_slabs:]
    step = pl.program_id(1)
    page = k_refs[0].shape[0]
    lanes = h * nq
    heads = [slice(hh * HEAD_DIM, (hh + 1) * HEAD_DIM) for hh in range(h)]

    def visit(n, mask):
        z = _dot(kc_ref[0:n, :], qbd_ref[...], NT_DIMS) + bias_ref[...]
        log_rest = -_softplus(z)
        if mask is not None:
            log_rest = jnp.where(mask, log_rest, 0.0)
        ri = lax.broadcasted_iota(jnp.int32, (n, n), 0)
        ci = lax.broadcasted_iota(jnp.int32, (n, n), 1)
        suffix = (ci >= ri).astype(BF16)
        hi, lo = _bf16_split(log_rest)
        csum = _dot(suffix, hi) + _dot(suffix, lo)
        a = jnp.exp(z + csum + r_ref[...])
        if mask is not None:
            a = jnp.where(mask, a, 0.0)
        acc_ref[...] += _dot(jnp.transpose(a).astype(BF16), vc_ref[0:n, :])
        r_ref[...] += csum[0:1, :]

    @pl.when(step == 0)
    def _():
        qs = q_ref[...] * scale
        qt = jnp.broadcast_to(qs[None], (h, nq, d)).reshape(lanes, d)
        rr = lax.broadcasted_iota(jnp.int32, (lanes, d), 0)
        cc = lax.broadcasted_iota(jnp.int32, (lanes, d), 1)
        qbd_ref[...] = jnp.where(rr // nq == cc // HEAD_DIM, qt, 0.0).astype(BF16)
        acc_ref[...] = jnp.zeros_like(acc_ref)
        r_ref[...] = jnp.zeros_like(r_ref)
        pad = jnp.zeros((page - nq, d), BF16)
        kc_ref[0:page, :] = jnp.concatenate([kn_ref[...].astype(BF16), pad], axis=0)
        vc_ref[0:page, :] = jnp.concatenate([vn_ref[...].astype(BF16), pad], axis=0)
        key = lax.broadcasted_iota(jnp.int32, (page, lanes), 0)
        query = lax.broadcasted_iota(jnp.int32, (page, lanes), 1) % nq
        visit(page, key < query)

    @pl.when(step > 0)
    def _():
        for p in range(pages):
            rows = slice((pages - 1 - p) * page, (pages - p) * page)
            for g in range(groups):
                for hh in range(SUBLANES):
                    cols = heads[g * SUBLANES + hh]
                    kc_ref[rows, cols] = _slab_head(k_refs[p * groups + g], hh).astype(BF16)
                    vc_ref[rows, cols] = _slab_head(v_refs[p * groups + g], hh).astype(BF16)
        visit(pages * page, None)

    @pl.when(step == pl.num_programs(1) - 1)
    def _():
        for hh, cols in enumerate(heads):
            o_ref[:, cols] = acc_ref[hh * nq:(hh + 1) * nq, cols]


def _sb_sample(q, k_new, v_new, bias, cache_k, cache_v, page_table):
    b, nq, d = q.shape
    n_pool, page, h, dh = cache_k.shape
    n_pages = page_table.shape[1]
    pages = min(SB_PAGES_PER_STEP, n_pages)
    assert n_pages % pages == 0 and h * nq == HEAD_DIM and nq <= page and dh == HEAD_DIM and h * dh == d and h % SUBLANES == 0
    lanes = h * nq

    def slab_spec(p, g):
        def index(bi, si, pt):
            group = jnp.maximum(si - 1, 0)
            return (pt[bi * n_pages + n_pages - 1 - (group * pages + p)], 0, g, 0)
        return pl.BlockSpec((None, page, SUBLANES, dh), index)

    slab_specs = [slab_spec(p, g) for p in range(pages) for g in range(h // SUBLANES)]

    tok_spec = pl.BlockSpec((None, nq, d), lambda bi, si, pt: (bi, 0, 0))
    scratch = [
        pltpu.VMEM((lanes, d), BF16),
        pltpu.VMEM((pages * page, d), BF16),
        pltpu.VMEM((pages * page, d), BF16),
        pltpu.VMEM((lanes, d), F32),
        pltpu.VMEM((1, lanes), F32),
    ]
    blocks = 2 * pages * _nbytes((page, d), F32) + 4 * _nbytes((nq, d), F32)
    scratch_bytes = (_nbytes((lanes, d), BF16) + 2 * _nbytes((pages * page, d), BF16)
                     + _nbytes((lanes, d), F32))
    grid_spec = pltpu.PrefetchScalarGridSpec(
        num_scalar_prefetch=1,
        grid=(b, 1 + n_pages // pages),
        in_specs=[tok_spec, tok_spec, tok_spec,
                  pl.BlockSpec((1, lanes), lambda bi, si, pt: (0, 0))]
                 + slab_specs * 2,
        out_specs=tok_spec,
        scratch_shapes=scratch,
    )
    return pl.pallas_call(
        functools.partial(_sb_sample_kernel, pages=pages, scale=HEAD_DIM ** -0.5),
        grid_spec=grid_spec,
        out_shape=jax.ShapeDtypeStruct((b, nq, d), F32),
        compiler_params=_params(("parallel", "arbitrary"), blocks, scratch_bytes),
        name="sb_sample",
    )(page_table.reshape(-1), q, k_new, v_new, jnp.repeat(bias, nq).reshape(1, lanes),
      *([cache_k] * len(slab_specs)), *([cache_v] * len(slab_specs)))


def _run_trunk(x3, p, hgrn_state, conv_state, attend, w, *, hgrn_bb, hgrn_hp, hgrn_tc, conv_bb):
    b, t, d = x3.shape
    m = b * t
    depth = w["norm_mix"].shape[0]
    n_a = w["a_w_in"].shape[0]
    f = w["ffn_w_out"].shape[1]
    x = x3.reshape(m, d)
    new_s, new_c = [], []
    k_sh = v_sh = None
    for layer in range(depth):
        if layer < n_a:
            proj = _norm_matmul(x, w["norm_mix"][layer], w["a_w_in"][layer], n_out=4 * d)
            og, s = _hgrn(proj.reshape(b, t, 4 * d), w["a_lb"], w["a_g_norm"][layer], hgrn_state[layer],
                          layer=layer, bb=hgrn_bb, hp=hgrn_hp, tc=hgrn_tc)
            new_s.append(s)
            x = _matmul_res(og.reshape(m, d), w["a_w_out"][layer], x, tn=1024)
        else:
            bi = layer - n_a
            q = _norm_matmul(x, w["norm_mix"][layer], w["b_w_q"][bi], n_out=d, head_g=w["b_q_norm"][bi])
            o = attend(q.reshape(b, t, d), k_sh, v_sh, w["b_sb_bias"][bi])
            x = _matmul_res(o.reshape(m, d), w["b_w_out"][bi], x, tn=1024)
        u = _norm_matmul(x, w["norm_ffn"][layer], w["ffn_w_in"][layer], n_out=2 * f)
        act, cbuf = _conv_gate(u.reshape(b, t, 2 * f), conv_state[layer], w["ffn_conv_w"][layer],
                               w["ffn_conv_b"][layer], bb=conv_bb)
        new_c.append(cbuf)
        x = _matmul_res(act.reshape(m, f), w["ffn_w_out"][layer], x)
        x = _ple(x, p[layer].reshape(m, -1), w["ple_norm"][layer], w["ple_w"][layer], w["ple_w_gate"][layer])
        if layer == n_a - 1:
            k_sh = _norm_matmul(x, w["kv_norm"], w["kv_w"], n_out=d, head_g=w["k_norm"]).reshape(b, t, d)
            v_sh = _norm_matmul(x, w["kv_norm"], w["kv_w"], n_out=d, col_off=d).reshape(b, t, d)
    return x.reshape(b, t, d), k_sh, v_sh, jnp.stack(new_s), jnp.stack(new_c)


def kernel(x_prompt, x_sample, p_prompt, p_sample, cache_k, cache_v, page_table, state_hgrn, state_conv, norm_mix, norm_ffn, a_w_in, a_lb, a_g_norm, a_w_out, kv_norm, kv_w, k_norm, b_w_q, b_q_norm, b_sb_bias, b_w_out, ffn_w_in, ffn_conv_w, ffn_conv_b, ffn_w_out, ple_w, ple_norm, ple_w_gate):
    w = dict(norm_mix=norm_mix, norm_ffn=norm_ffn, a_lb=a_lb, a_g_norm=a_g_norm, kv_norm=kv_norm,
             k_norm=k_norm, b_q_norm=b_q_norm, b_sb_bias=b_sb_bias, ffn_conv_w=ffn_conv_w,
             ffn_conv_b=ffn_conv_b, ple_norm=ple_norm)
    for name, val in dict(a_w_in=a_w_in, a_w_out=a_w_out, kv_w=kv_w, b_w_q=b_w_q, b_w_out=b_w_out,
                          ffn_w_in=ffn_w_in, ffn_w_out=ffn_w_out, ple_w=ple_w, ple_w_gate=ple_w_gate).items():
        w[name] = val.astype(BF16)

    b, t, d = x_prompt.shape
    heads = d // HEAD_DIM
    n_a = a_w_in.shape[0]
    depth = norm_mix.shape[0]
    f = ffn_w_out.shape[1]
    zero_h = jnp.zeros((n_a, b, heads, HEAD_DIM, HEAD_DIM), state_hgrn.dtype)
    zero_c = jnp.zeros((depth, b, state_conv.shape[2], f), state_conv.dtype)

    def attend_prompt(q, k, v, bias):
        return _sb_prompt(q, k, v, bias)

    def attend_sample(q, k, v, bias):
        return _sb_sample(q, k, v, bias, cache_k, cache_v, page_table)

    y_p, k_p, v_p, h_p, c_p = _run_trunk(x_prompt, p_prompt, zero_h, zero_c, attend_prompt, w,
                                         hgrn_bb=1, hgrn_hp=4, hgrn_tc=512, conv_bb=1)
    y_s, k_s, v_s, h_s, c_s = _run_trunk(x_sample, p_sample, state_hgrn, state_conv, attend_sample, w,
                                         hgrn_bb=8, hgrn_hp=1, hgrn_tc=x_sample.shape[1], conv_bb=x_sample.shape[0])
    hs = (heads, HEAD_DIM)
    return (y_p, y_s,
            k_p.reshape(k_p.shape[:2] + hs), v_p.reshape(v_p.shape[:2] + hs),
            k_s.reshape(k_s.shape[:2] + hs), v_s.reshape(v_s.shape[:2] + hs),
            h_p, h_s, c_p, c_s)
```

```python
import functools

import jax
import jax.numpy as jnp
from jax import lax
from jax.experimental import pallas as pl
from jax.experimental.pallas import tpu as pltpu

F32 = jnp.float32
BF16 = jnp.bfloat16
EPS = 1e-6
LOG2E = 1.4426950408889634
HEAD_DIM = 128
SUBLANES = 8
HGRN_CHUNK = 64
HGRN_SUB = 16
SB_BLOCK_Q = 512
SB_BLOCK_K = 512
SB_CUMSUM_BLOCK = 256
SB_PAGES_PER_STEP = 4
SB_HEADS_PER_STEP = 2
V7X_VMEM_CAP = 56 * 1024 * 1024
VMEM_TEMP_MARGIN = 16 * 1024 * 1024

NT_DIMS = (((1,), (1,)), ((), ()))
TN_DIMS = (((0,), (0,)), ((), ()))


def _params(semantics, block_bytes, scratch_bytes=0):
    limit = min(2 * block_bytes + scratch_bytes + VMEM_TEMP_MARGIN, V7X_VMEM_CAP)
    return pltpu.CompilerParams(dimension_semantics=semantics, vmem_limit_bytes=limit)


def _nbytes(shape, dtype):
    n = jnp.dtype(dtype).itemsize
    for s in shape:
        n *= s
    return n


def _rms(x, g):
    return x * lax.rsqrt(jnp.mean(x * x, axis=-1, keepdims=True) + EPS) * g


def _silu(x):
    return x * jax.nn.sigmoid(x)


def _softplus(z):
    return jnp.maximum(z, 0.0) + jnp.log(1.0 + jnp.exp(-jnp.abs(z)))


def _softplus2(z2):
    return jnp.maximum(z2, 0.0) + jnp.log2(1.0 + jnp.exp2(-jnp.abs(z2)))


def _bf16_split(x):
    hi = x.astype(BF16)
    lo = (x - hi.astype(F32)).astype(BF16)
    return hi, lo


def _dot(a, b, dims=None):
    if dims is None:
        return jnp.dot(a, b, preferred_element_type=F32)
    return lax.dot_general(a, b, dims, preferred_element_type=F32)


def _norm_matmul_kernel(x_ref, g_ref, w_ref, *rest, head_norm):
    if head_norm:
        hg_ref, o_ref, xn_ref = rest
    else:
        o_ref, xn_ref = rest

    @pl.when(pl.program_id(1) == 0)
    def _():
        xn_ref[...] = _rms(x_ref[...], g_ref[...]).astype(BF16)

    acc = _dot(xn_ref[...], w_ref[...])
    if head_norm:
        for c in range(acc.shape[1] // HEAD_DIM):
            cols = slice(c * HEAD_DIM, (c + 1) * HEAD_DIM)
            o_ref[:, cols] = _rms(acc[:, cols], hg_ref[...])
    else:
        o_ref[...] = acc


def _norm_matmul(x, g, w, *, n_out, head_g=None, tm=1024, tn=1024):
    m, k = x.shape
    tm, tn = min(tm, m), min(tn, n_out)
    assert m % tm == 0 and n_out % tn == 0 and w.shape[1] == n_out
    in_specs = [
        pl.BlockSpec((tm, k), lambda i, j: (i, 0)),
        pl.BlockSpec((1, k), lambda i, j: (0, 0)),
        pl.BlockSpec((k, tn), lambda i, j: (0, j)),
    ]
    args = [x, g.reshape(1, k), w]
    if head_g is not None:
        in_specs.append(pl.BlockSpec((1, HEAD_DIM), lambda i, j: (0, 0)))
        args.append(head_g.reshape(1, HEAD_DIM))
    blocks = _nbytes((tm, k), F32) + _nbytes((k, tn), BF16) + _nbytes((tm, tn), F32)
    return pl.pallas_call(
        functools.partial(_norm_matmul_kernel, head_norm=head_g is not None),
        grid=(m // tm, n_out // tn),
        in_specs=in_specs,
        out_specs=pl.BlockSpec((tm, tn), lambda i, j: (i, j)),
        out_shape=jax.ShapeDtypeStruct((m, n_out), F32),
        scratch_shapes=[pltpu.VMEM((tm, k), BF16)],
        compiler_params=_params(("parallel", "arbitrary"), blocks, _nbytes((tm, k), BF16)),
        name="norm_matmul_headnorm" if head_g is not None else "norm_matmul",
    )(*args)


def _kv_proj_kernel(x_ref, g_ref, wk_ref, wv_ref, hg_ref, k_ref, v_ref, xn_ref):
    @pl.when(pl.program_id(1) == 0)
    def _():
        xn_ref[...] = _rms(x_ref[...], g_ref[...]).astype(BF16)

    xn = xn_ref[...]
    acc = _dot(xn, wk_ref[...])
    for c in range(acc.shape[1] // HEAD_DIM):
        cols = slice(c * HEAD_DIM, (c + 1) * HEAD_DIM)
        k_ref[:, cols] = _rms(acc[:, cols], hg_ref[...])
    v_ref[...] = _dot(xn, wv_ref[...])


def _kv_proj(x, g, w, head_g, *, tm=1024, tn=512):
    m, d = x.shape
    n = w.shape[1] // 2
    tm, tn = min(tm, m), min(tn, n)
    assert m % tm == 0 and n % tn == 0 and tn % HEAD_DIM == 0
    nj = n // tn
    out_spec = pl.BlockSpec((tm, tn), lambda i, j: (i, j))
    blocks = _nbytes((tm, d), F32) + 2 * _nbytes((d, tn), BF16) + 2 * _nbytes((tm, tn), F32)
    return pl.pallas_call(
        _kv_proj_kernel,
        grid=(m // tm, nj),
        in_specs=[
            pl.BlockSpec((tm, d), lambda i, j: (i, 0)),
            pl.BlockSpec((1, d), lambda i, j: (0, 0)),
            pl.BlockSpec((d, tn), lambda i, j: (0, j)),
            pl.BlockSpec((d, tn), lambda i, j: (0, j + nj)),
            pl.BlockSpec((1, HEAD_DIM), lambda i, j: (0, 0)),
        ],
        out_specs=[out_spec, out_spec],
        out_shape=[jax.ShapeDtypeStruct((m, n), F32)] * 2,
        scratch_shapes=[pltpu.VMEM((tm, d), BF16)],
        compiler_params=_params(("parallel", "arbitrary"), blocks, _nbytes((tm, d), BF16)),
        name="kv_proj",
    )(x, g.reshape(1, d), w, w, head_g.reshape(1, HEAD_DIM))


def _ffn_in_kernel(x_ref, g_ref, wg_ref, wu_ref, buf_ref, cw_ref, cb_ref, act_ref, st_ref, xn_ref, tail_ref,
                   *, tiles_per_seq):
    i, j = pl.program_id(0), pl.program_id(1)
    tm = x_ref.shape[0]

    @pl.when(j == 0)
    def _():
        xn_ref[...] = _rms(x_ref[...], g_ref[...]).astype(BF16)

    @pl.when(i % tiles_per_seq == 0)
    def _():
        tail_ref[j, SUBLANES - 2:SUBLANES, :] = buf_ref[...]

    xn = xn_ref[...]
    ug = _dot(xn, wg_ref[...])
    t = lax.broadcasted_iota(jnp.int32, ug.shape, 0)
    h1 = tail_ref[j, SUBLANES - 1:SUBLANES, :]
    h2 = tail_ref[j, SUBLANES - 2:SUBLANES - 1, :]
    xm1 = jnp.where(t == 0, h1, pltpu.roll(ug, 1, axis=0))
    xm2 = jnp.where(t == 0, h2, jnp.where(t == 1, h1, pltpu.roll(ug, 2, axis=0)))
    conv = cb_ref[...] + xm2 * cw_ref[0:1, :] + xm1 * cw_ref[1:2, :] + ug * cw_ref[2:3, :]
    act_ref[...] = (_silu(conv) * _dot(xn, wu_ref[...])).astype(act_ref.dtype)
    tail_ref[j] = ug[tm - SUBLANES:tm, :]
    st_ref[...] = ug[tm - 2:tm, :]


def _ffn_in(x, g, w, buf, conv_w, conv_b, *, seq_len, tm=1024, tn=512):
    m, d = x.shape
    f = w.shape[1] // 2
    tm = min(tm, seq_len)
    assert seq_len % tm == 0 and m % seq_len == 0 and f % tn == 0 and buf.shape[1] == 2 and tm >= SUBLANES
    tiles_per_seq = seq_len // tm
    nj = f // tn
    blocks = (_nbytes((tm, d), F32) + 2 * _nbytes((d, tn), BF16) + _nbytes((tm, tn), BF16)
              + 2 * _nbytes((2, tn), F32))
    scratch_bytes = _nbytes((tm, d), BF16) + _nbytes((nj, SUBLANES, tn), F32)
    act, tails = pl.pallas_call(
        functools.partial(_ffn_in_kernel, tiles_per_seq=tiles_per_seq),
        grid=(m // tm, nj),
        in_specs=[
            pl.BlockSpec((tm, d), lambda i, j: (i, 0)),
            pl.BlockSpec((1, d), lambda i, j: (0, 0)),
            pl.BlockSpec((d, tn), lambda i, j: (0, j)),
            pl.BlockSpec((d, tn), lambda i, j: (0, j + nj)),
            pl.BlockSpec((None, 2, tn), lambda i, j: (i // tiles_per_seq, 0, j)),
            pl.BlockSpec((3, tn), lambda i, j: (0, j)),
            pl.BlockSpec((1, tn), lambda i, j: (0, j)),
        ],
        out_specs=[
            pl.BlockSpec((tm, tn), lambda i, j: (i, j)),
            pl.BlockSpec((None, 2, tn), lambda i, j: (i, 0, j)),
        ],
        out_shape=[
            jax.ShapeDtypeStruct((m, f), BF16),
            jax.ShapeDtypeStruct((m // tm, 2, f), F32),
        ],
        scratch_shapes=[pltpu.VMEM((tm, d), BF16), pltpu.VMEM((nj, SUBLANES, tn), F32)],
        compiler_params=_params(("arbitrary", "arbitrary"), blocks, scratch_bytes),
        name="ffn_in",
    )(x, g.reshape(1, d), w, w, buf, conv_w, conv_b.reshape(1, f))
    return act, tails[tiles_per_seq - 1::tiles_per_seq]


def _matmul_res_kernel(a_ref, w_ref, r_ref, o_ref):
    o_ref[...] = r_ref[...] + _dot(a_ref[...].astype(BF16), w_ref[...])


def _matmul_res(a, w, res, *, tm=1024, tn=512):
    m, k = a.shape
    n = w.shape[1]
    tm, tn = min(tm, m), min(tn, n)
    assert m % tm == 0 and n % tn == 0
    blocks = _nbytes((tm, k), a.dtype) + _nbytes((k, tn), BF16) + 2 * _nbytes((tm, tn), F32)
    return pl.pallas_call(
        _matmul_res_kernel,
        grid=(m // tm, n // tn),
        in_specs=[
            pl.BlockSpec((tm, k), lambda i, j: (i, 0)),
            pl.BlockSpec((k, tn), lambda i, j: (0, j)),
            pl.BlockSpec((tm, tn), lambda i, j: (i, j)),
        ],
        out_specs=pl.BlockSpec((tm, tn), lambda i, j: (i, j)),
        out_shape=jax.ShapeDtypeStruct((m, n), F32),
        compiler_params=_params(("parallel", "arbitrary"), blocks),
        name="matmul_res",
    )(a, w, res)


def _ple_kernel(x_ref, xr_ref, p_ref, g_ref, wpe_ref, wg_ref, o_ref, xn_ref):
    @pl.when(pl.program_id(1) == 0)
    def _():
        xn_ref[...] = _rms(x_ref[...], g_ref[...]).astype(BF16)

    gate = jax.nn.sigmoid(_dot(xn_ref[...], wg_ref[...]))
    pe = _dot(p_ref[...].astype(BF16), wpe_ref[...])
    o_ref[...] = xr_ref[...] + pe * gate


def _ple(x, p, g, w_pe, w_gate, *, tm=512, tn=1024):
    m, d = x.shape
    pd = p.shape[1]
    tm, tn = min(tm, m), min(tn, d)
    assert m % tm == 0 and d % tn == 0
    blocks = (_nbytes((tm, d), F32) + 2 * _nbytes((tm, tn), F32) + _nbytes((tm, pd), F32)
              + _nbytes((pd, tn), BF16) + _nbytes((d, tn), BF16))
    return pl.pallas_call(
        _ple_kernel,
        grid=(m // tm, d // tn),
        in_specs=[
            pl.BlockSpec((tm, d), lambda i, j: (i, 0)),
            pl.BlockSpec((tm, tn), lambda i, j: (i, j)),
            pl.BlockSpec((tm, pd), lambda i, j: (i, 0)),
            pl.BlockSpec((1, d), lambda i, j: (0, 0)),
            pl.BlockSpec((pd, tn), lambda i, j: (0, j)),
            pl.BlockSpec((d, tn), lambda i, j: (0, j)),
        ],
        out_specs=pl.BlockSpec((tm, tn), lambda i, j: (i, j)),
        out_shape=jax.ShapeDtypeStruct((m, d), F32),
        scratch_shapes=[pltpu.VMEM((tm, d), BF16)],
        compiler_params=_params(("parallel", "arbitrary"), blocks, _nbytes((tm, d), BF16)),
        name="ple",
    )(x, x, p, g.reshape(1, d), w_pe, w_gate)


def _conv_gate_kernel(ug_ref, uu_ref, buf_ref, cw_ref, cb_ref, act_ref, st_ref):
    x = ug_ref[...]
    t_len = x.shape[1]
    t = lax.broadcasted_iota(jnp.int32, x.shape, 1)
    buf0 = buf_ref[:, 0:1, :]
    buf1 = buf_ref[:, 1:2, :]
    xm1 = jnp.where(t == 0, buf1, pltpu.roll(x, 1, axis=1))
    xm2 = jnp.where(t == 0, buf0, jnp.where(t == 1, buf1, pltpu.roll(x, 2, axis=1)))
    conv = cb_ref[...] + xm2 * cw_ref[0:1, :] + xm1 * cw_ref[1:2, :] + x * cw_ref[2:3, :]
    act_ref[...] = (_silu(conv) * uu_ref[...]).astype(act_ref.dtype)
    st_ref[...] = ug_ref[:, t_len - 2:t_len, :]


def _conv_gate(u, buf, conv_w, conv_b, *, bb, tk=512):
    b, t, f2 = u.shape
    f = f2 // 2
    assert b % bb == 0 and f % tk == 0 and t >= 2 and buf.shape[1] == 2
    nk = f // tk
    blocks = 2 * _nbytes((bb, t, tk), F32) + _nbytes((bb, t, tk), BF16) + 2 * _nbytes((bb, 2, tk), F32)
    return pl.pallas_call(
        _conv_gate_kernel,
        grid=(b // bb, nk),
        in_specs=[
            pl.BlockSpec((bb, t, tk), lambda i, j: (i, 0, j)),
            pl.BlockSpec((bb, t, tk), lambda i, j: (i, 0, j + nk)),
            pl.BlockSpec((bb, 2, tk), lambda i, j: (i, 0, j)),
            pl.BlockSpec((3, tk), lambda i, j: (0, j)),
            pl.BlockSpec((1, tk), lambda i, j: (0, j)),
        ],
        out_specs=[
            pl.BlockSpec((bb, t, tk), lambda i, j: (i, 0, j)),
            pl.BlockSpec((bb, 2, tk), lambda i, j: (i, 0, j)),
        ],
        out_shape=[
            jax.ShapeDtypeStruct((b, t, f), BF16),
            jax.ShapeDtypeStruct((b, 2, f), F32),
        ],
        compiler_params=_params(("parallel", "arbitrary"), blocks),
        name="conv_gate",
    )(u, u, buf, conv_w, conv_b.reshape(1, f))


def _cumsum_rows(x):
    n = x.shape[0]
    row = lax.broadcasted_iota(jnp.int32, x.shape, 0)
    s = 1
    while s < n:
        x = x + jnp.where(row >= s, pltpu.roll(x, s, axis=0), 0.0)
        s *= 2
    return x


def _hgrn_chunk(qz, fz, iv, state, lb, sub):
    c = qz.shape[0]
    f = lb + (1.0 - lb) * jax.nn.sigmoid(fz)
    q = _silu(qz)
    k = 1.0 - f
    b = _cumsum_rows(jnp.log(f))
    b_last = b[c - 1:c, :]
    o = _dot((q * jnp.exp(b)).astype(BF16), state.astype(BF16))

    lane = lax.broadcasted_iota(jnp.int32, (sub, c), 1)
    rowi = lax.broadcasted_iota(jnp.int32, (sub, c), 0)
    att_rows = []
    for i in range(c // sub):
        r0 = i * sub
        bi, qi, ki = b[r0:r0 + sub], q[r0:r0 + sub], k[r0:r0 + sub]
        att = jnp.zeros((sub, c), F32)
        for s in range(sub):
            pair = qi * ki[s:s + 1] * jnp.exp(jnp.minimum(bi - bi[s:s + 1], 0.0))
            col = jnp.sum(pair, axis=1, keepdims=True)
            att = jnp.where((lane == r0 + s) & (rowi >= s), col, att)
        if i > 0:
            mid = bi[0:1]
            qt = qi * jnp.exp(bi - mid)
            kt = k[:r0] * jnp.exp(mid - b[:r0])
            kt = jnp.concatenate([kt, jnp.zeros((c - r0, HEAD_DIM), F32)], axis=0)
            att = att + _dot(qt.astype(BF16), kt.astype(BF16), NT_DIMS)
        att_rows.append(att)
    att = att_rows[0] if len(att_rows) == 1 else jnp.concatenate(att_rows, axis=0)
    o = o + _dot(att.astype(BF16), iv.astype(BF16))

    decay = jnp.transpose(jnp.broadcast_to(jnp.exp(b_last), (HEAD_DIM, HEAD_DIM)))
    k_st = k * jnp.exp(b_last - b)
    new_state = decay * state + _dot(k_st.astype(BF16), iv.astype(BF16), TN_DIMS)
    return o, new_state


def _hgrn_kernel(q_ref, f_ref, i_ref, g_ref, alb_ref, gn_ref, s0_ref, og_ref, sout_ref, s_ref,
                 *, layer, chunk, sub):
    t_step = pl.program_id(2)

    @pl.when(t_step == 0)
    def _():
        s_ref[...] = s0_ref[...]

    a = alb_ref[...]
    e = jnp.exp(a - jnp.max(a, axis=0, keepdims=True))
    sm = e / jnp.sum(e, axis=0, keepdims=True)
    lb = jnp.sum(sm[:layer + 1], axis=0, keepdims=True)

    bb, tc, width = q_ref.shape

    def chunk_body(ci, carry):
        rows = pl.ds(pl.multiple_of(ci * chunk, chunk), chunk)
        for bi in range(bb):
            for hh in range(width // HEAD_DIM):
                cols = slice(hh * HEAD_DIM, (hh + 1) * HEAD_DIM)
                o, state = _hgrn_chunk(q_ref[bi, rows, cols], f_ref[bi, rows, cols], i_ref[bi, rows, cols],
                                       s_ref[bi, hh], lb[:, cols], sub)
                og_ref[bi, rows, cols] = _rms(o, gn_ref[...]) * _silu(g_ref[bi, rows, cols])
                s_ref[bi, hh] = state
        return carry

    lax.fori_loop(0, tc // chunk, chunk_body, 0)

    @pl.when(t_step == pl.num_programs(2) - 1)
    def _():
        sout_ref[...] = s_ref[...]


def _hgrn(proj, a_lb, g_norm, s0, *, layer, bb, hp, tc):
    b, t, d4 = proj.shape
    d = d4 // 4
    h = d // HEAD_DIM
    chunk = min(HGRN_CHUNK, t)
    sub = min(HGRN_SUB, chunk)
    assert b % bb == 0 and h % hp == 0 and t % tc == 0 and tc % chunk == 0 and chunk % sub == 0
    nl = a_lb.shape[0]
    width = hp * HEAD_DIM
    hg = h // hp

    def col(off):
        return pl.BlockSpec((bb, tc, width), lambda bi, hi, ti: (bi, ti, hi + off * hg))

    state_shape = (bb, hp, HEAD_DIM, HEAD_DIM)
    state_spec = pl.BlockSpec(state_shape, lambda bi, hi, ti: (bi, hi, 0, 0))
    blocks = 5 * _nbytes((bb, tc, width), F32) + 2 * _nbytes(state_shape, F32)
    return pl.pallas_call(
        functools.partial(_hgrn_kernel, layer=layer, chunk=chunk, sub=sub),
        grid=(b // bb, hg, t // tc),
        in_specs=[
            col(0), col(1), col(2), col(3),
            pl.BlockSpec((nl, width), lambda bi, hi, ti: (0, hi)),
            pl.BlockSpec((1, HEAD_DIM), lambda bi, hi, ti: (0, 0)),
            state_spec,
        ],
        out_specs=[
            pl.BlockSpec((bb, tc, width), lambda bi, hi, ti: (bi, ti, hi)),
            state_spec,
        ],
        out_shape=[
            jax.ShapeDtypeStruct((b, t, d), F32),
            jax.ShapeDtypeStruct(s0.shape, F32),
        ],
        scratch_shapes=[pltpu.VMEM(state_shape, F32)],
        compiler_params=_params(("parallel", "parallel", "arbitrary"), blocks, _nbytes(state_shape, F32)),
        name="hgrn",
    )(proj, proj, proj, proj, a_lb, g_norm.reshape(1, HEAD_DIM), s0)


def _sb_prompt_kernel(bias_ref, q_ref, k_ref, v_ref, o_ref, *, bq, bk, scale):
    group = pl.program_id(1)
    qi = pl.program_id(2)
    hp = q_ref.shape[1] // HEAD_DIM
    nsub = bq // bk
    heads = [slice(hh * HEAD_DIM, (hh + 1) * HEAD_DIM) for hh in range(hp)]
    bias = [bias_ref[group * hp + hh] * LOG2E for hh in range(hp)]
    q = [(q_ref[:, cols] * (scale * LOG2E)).astype(BF16) for cols in heads]
    cw = min(SB_CUMSUM_BLOCK, bk)
    row = lax.broadcasted_iota(jnp.int32, (cw, cw), 0)
    colm = lax.broadcasted_iota(jnp.int32, (cw, cw), 1)
    suffix = (row >= colm).astype(BF16)
    q_pos = lax.broadcasted_iota(jnp.int32, (bq, bk), 0)
    k_pos = lax.broadcasted_iota(jnp.int32, (bq, bk), 1)

    def visit(kb, carry, mask):
        rows = pl.ds(pl.multiple_of(kb * bk, bk), bk)
        out = []
        for hh, cols in enumerate(heads):
            acc, r = carry[hh]
            k = k_ref[rows, cols].astype(BF16)
            v = v_ref[rows, cols].astype(BF16)
            z = _dot(q[hh], k, NT_DIMS) + bias[hh]
            log_rest = -_softplus2(z)
            if mask is not None:
                log_rest = jnp.where(mask, log_rest, 0.0)
            hi, lo = _bf16_split(log_rest)
            parts, newer = [], r
            for cb in reversed(range(bk // cw)):
                kcols = slice(cb * cw, (cb + 1) * cw)
                csum = _dot(hi[:, kcols], suffix) + _dot(lo[:, kcols], suffix) + newer
                parts.insert(0, csum)
                newer = csum[:, 0:1]
            a = jnp.exp2(z + (parts[0] if len(parts) == 1 else jnp.concatenate(parts, axis=1)))
            if mask is not None:
                a = jnp.where(mask, a, 0.0)
            out.append((acc + _dot(a.astype(BF16), v), newer))
        return tuple(out)

    carry = ((jnp.zeros((bq, HEAD_DIM), F32), jnp.zeros((bq, 1), F32)),) * hp
    for sub in reversed(range(nsub)):
        carry = visit(qi * nsub + sub, carry, k_pos + sub * bk < q_pos)
    carry = lax.fori_loop(0, qi * nsub, lambda step, c: visit(qi * nsub - 1 - step, c, None), carry)
    for hh, cols in enumerate(heads):
        o_ref[:, cols] = carry[hh][0].astype(o_ref.dtype)


def _sb_prompt(q, k, v, bias):
    b, t, d = q.shape
    bq, bk = min(SB_BLOCK_Q, t), min(SB_BLOCK_K, t)
    width = SB_HEADS_PER_STEP * HEAD_DIM
    assert t % bq == 0 and bq % bk == 0 and d % width == 0
    kv_spec = pl.BlockSpec((None, t, width), lambda bi, hi, qi: (bi, 0, hi))
    q_spec = pl.BlockSpec((None, bq, width), lambda bi, hi, qi: (bi, qi, hi))
    blocks = 2 * _nbytes((t, width), F32) + _nbytes((bq, width), F32) + _nbytes((bq, width), BF16)
    return pl.pallas_call(
        functools.partial(_sb_prompt_kernel, bq=bq, bk=bk, scale=HEAD_DIM ** -0.5),
        grid=(b, d // width, t // bq),
        in_specs=[pl.BlockSpec(memory_space=pltpu.SMEM), q_spec, kv_spec, kv_spec],
        out_specs=q_spec,
        out_shape=jax.ShapeDtypeStruct((b, t, d), BF16),
        compiler_params=_params(("parallel", "parallel", "arbitrary"), blocks),
        name="sb_prompt",
    )(bias, q, k, v)


def _slab_head(slab_ref, hh):
    keys, nh, dim = slab_ref.shape
    return slab_ref.reshape(keys * nh, dim)[pl.ds(hh, keys, stride=nh), :]


def _sb_sample_kernel(pt_ref, q_ref, kn_ref, vn_ref, bias_ref, *rest, pages, scale):
    nq, d = q_ref.shape
    h = d // HEAD_DIM
    groups = h // SUBLANES
    n_slabs = pages * groups
    k_refs, v_refs = rest[:n_slabs], rest[n_slabs:2 * n_slabs]
    o_ref, qbd_ref, kc_ref, vc_ref, acc_ref, r_ref = rest[2 * n_slabs:]
    step = pl.program_id(1)
    page = k_refs[0].shape[0]
    lanes = h * nq
    heads = [slice(hh * HEAD_DIM, (hh + 1) * HEAD_DIM) for hh in range(h)]

    def visit(n, mask):
        z = _dot(kc_ref[0:n, :], qbd_ref[...], NT_DIMS) + bias_ref[...]
        log_rest = -_softplus(z)
        if mask is not None:
            log_rest = jnp.where(mask, log_rest, 0.0)
        ri = lax.broadcasted_iota(jnp.int32, (n, n), 0)
        ci = lax.broadcasted_iota(jnp.int32, (n, n), 1)
        suffix = (ci >= ri).astype(BF16)
        hi, lo = _bf16_split(log_rest)
        csum = _dot(suffix, hi) + _dot(suffix, lo)
        a = jnp.exp(z + csum + r_ref[...])
        if mask is not None:
            a = jnp.where(mask, a, 0.0)
        acc_ref[...] += _dot(jnp.transpose(a).astype(BF16), vc_ref[0:n, :])
        r_ref[...] += csum[0:1, :]

    @pl.when(step == 0)
    def _():
        qs = q_ref[...] * scale
        qt = jnp.broadcast_to(qs[None], (h, nq, d)).reshape(lanes, d)
        rr = lax.broadcasted_iota(jnp.int32, (lanes, d), 0)
        cc = lax.broadcasted_iota(jnp.int32, (lanes, d), 1)
        qbd_ref[...] = jnp.where(rr // nq == cc // HEAD_DIM, qt, 0.0).astype(BF16)
        acc_ref[...] = jnp.zeros_like(acc_ref)
        r_ref[...] = jnp.zeros_like(r_ref)
        pad = jnp.zeros((page - nq, d), BF16)
        kc_ref[0:page, :] = jnp.concatenate([kn_ref[...].astype(BF16), pad], axis=0)
        vc_ref[0:page, :] = jnp.concatenate([vn_ref[...].astype(BF16), pad], axis=0)
        key = lax.broadcasted_iota(jnp.int32, (page, lanes), 0)
        query = lax.broadcasted_iota(jnp.int32, (page, lanes), 1) % nq
        visit(page, key < query)

    @pl.when(step > 0)
    def _():
        for p in range(pages):
            rows = slice((pages - 1 - p) * page, (pages - p) * page)
            for g in range(groups):
                for hh in range(SUBLANES):
                    cols = heads[g * SUBLANES + hh]
                    kc_ref[rows, cols] = _slab_head(k_refs[p * groups + g], hh).astype(BF16)
                    vc_ref[rows, cols] = _slab_head(v_refs[p * groups + g], hh).astype(BF16)
        visit(pages * page, None)

    @pl.when(step == pl.num_programs(1) - 1)
    def _():
        for hh, cols in enumerate(heads):
            o_ref[:, cols] = acc_ref[hh * nq:(hh + 1) * nq, cols]


def _sb_sample(q, k_new, v_new, bias, cache_k, cache_v, page_table):
    b, nq, d = q.shape
    n_pool, page, h, dh = cache_k.shape
    n_pages = page_table.shape[1]
    pages = min(SB_PAGES_PER_STEP, n_pages)
    assert n_pages % pages == 0 and h * nq == HEAD_DIM and nq <= page and dh == HEAD_DIM and h * dh == d and h % SUBLANES == 0
    lanes = h * nq

    def slab_spec(p, g):
        def index(bi, si, pt):
            group = jnp.maximum(si - 1, 0)
            return (pt[bi * n_pages + n_pages - 1 - (group * pages + p)], 0, g, 0)
        return pl.BlockSpec((None, page, SUBLANES, dh), index)

    slab_specs = [slab_spec(p, g) for p in range(pages) for g in range(h // SUBLANES)]

    tok_spec = pl.BlockSpec((None, nq, d), lambda bi, si, pt: (bi, 0, 0))
    scratch = [
        pltpu.VMEM((lanes, d), BF16),
        pltpu.VMEM((pages * page, d), BF16),
        pltpu.VMEM((pages * page, d), BF16),
        pltpu.VMEM((lanes, d), F32),
        pltpu.VMEM((1, lanes), F32),
    ]
    blocks = 2 * pages * _nbytes((page, d), F32) + 4 * _nbytes((nq, d), F32)
    scratch_bytes = (_nbytes((lanes, d), BF16) + 2 * _nbytes((pages * page, d), BF16)
                     + _nbytes((lanes, d), F32))
    grid_spec = pltpu.PrefetchScalarGridSpec(
        num_scalar_prefetch=1,
        grid=(b, 1 + n_pages // pages),
        in_specs=[tok_spec, tok_spec, tok_spec,
                  pl.BlockSpec((1, lanes), lambda bi, si, pt: (0, 0))]
                 + slab_specs * 2,
        out_specs=tok_spec,
        scratch_shapes=scratch,
    )
    return pl.pallas_call(
        functools.partial(_sb_sample_kernel, pages=pages, scale=HEAD_DIM ** -0.5),
        grid_spec=grid_spec,
        out_shape=jax.ShapeDtypeStruct((b, nq, d), F32),
        compiler_params=_params(("parallel", "arbitrary"), blocks, scratch_bytes),
        name="sb_sample",
    )(page_table.reshape(-1), q, k_new, v_new, jnp.repeat(bias, nq).reshape(1, lanes),
      *([cache_k] * len(slab_specs)), *([cache_v] * len(slab_specs)))


def _run_trunk(x3, p, hgrn_state, conv_state, attend, w, *, hgrn_bb, hgrn_hp, hgrn_tc, fuse_conv):
    b, t, d = x3.shape
    m = b * t
    depth = w["norm_mix"].shape[0]
    n_a = w["a_w_in"].shape[0]
    f = w["ffn_w_out"].shape[1]
    x = x3.reshape(m, d)
    new_s, new_c = [], []
    k_sh = v_sh = None
    for layer in range(depth):
        if layer < n_a:
            proj = _norm_matmul(x, w["norm_mix"][layer], w["a_w_in"][layer], n_out=4 * d)
            og, s = _hgrn(proj.reshape(b, t, 4 * d), w["a_lb"], w["a_g_norm"][layer], hgrn_state[layer],
                          layer=layer, bb=hgrn_bb, hp=hgrn_hp, tc=hgrn_tc)
            new_s.append(s)
            x = _matmul_res(og.reshape(m, d), w["a_w_out"][layer], x, tn=1024)
        else:
            bi = layer - n_a
            q = _norm_matmul(x, w["norm_mix"][layer], w["b_w_q"][bi], n_out=d, head_g=w["b_q_norm"][bi])
            o = attend(q.reshape(b, t, d), k_sh, v_sh, w["b_sb_bias"][bi])
            x = _matmul_res(o.reshape(m, d), w["b_w_out"][bi], x, tn=1024)
        if fuse_conv:
            act, cbuf = _ffn_in(x, w["norm_ffn"][layer], w["ffn_w_in"][layer], conv_state[layer],
                                w["ffn_conv_w"][layer], w["ffn_conv_b"][layer], seq_len=t)
        else:
            u = _norm_matmul(x, w["norm_ffn"][layer], w["ffn_w_in"][layer], n_out=2 * f)
            act, cbuf = _conv_gate(u.reshape(b, t, 2 * f), conv_state[layer], w["ffn_conv_w"][layer],
                                   w["ffn_conv_b"][layer], bb=b)
        new_c.append(cbuf)
        x = _matmul_res(act.reshape(m, f), w["ffn_w_out"][layer], x)
        x = _ple(x, p[layer].reshape(m, -1), w["ple_norm"][layer], w["ple_w"][layer], w["ple_w_gate"][layer])
        if layer == n_a - 1:
            k_sh, v_sh = _kv_proj(x, w["kv_norm"], w["kv_w"], w["k_norm"])
            k_sh, v_sh = k_sh.reshape(b, t, d), v_sh.reshape(b, t, d)
    return x.reshape(b, t, d), k_sh, v_sh, jnp.stack(new_s), jnp.stack(new_c)


def kernel(x_prompt, x_sample, p_prompt, p_sample, cache_k, cache_v, page_table, state_hgrn, state_conv, norm_mix, norm_ffn, a_w_in, a_lb, a_g_norm, a_w_out, kv_norm, kv_w, k_norm, b_w_q, b_q_norm, b_sb_bias, b_w_out, ffn_w_in, ffn_conv_w, ffn_conv_b, ffn_w_out, ple_w, ple_norm, ple_w_gate):
    w = dict(norm_mix=norm_mix, norm_ffn=norm_ffn, a_lb=a_lb, a_g_norm=a_g_norm, kv_norm=kv_norm,
             k_norm=k_norm, b_q_norm=b_q_norm, b_sb_bias=b_sb_bias, ffn_conv_w=ffn_conv_w,
             ffn_conv_b=ffn_conv_b, ple_norm=ple_norm)
    for name, val in dict(a_w_in=a_w_in, a_w_out=a_w_out, kv_w=kv_w, b_w_q=b_w_q, b_w_out=b_w_out,
                          ffn_w_in=ffn_w_in, ffn_w_out=ffn_w_out, ple_w=ple_w, ple_w_gate=ple_w_gate).items():
        w[name] = val.astype(BF16)

    b, t, d = x_prompt.shape
    heads = d // HEAD_DIM
    n_a = a_w_in.shape[0]
    depth = norm_mix.shape[0]
    f = ffn_w_out.shape[1]
    zero_h = jnp.zeros((n_a, b, heads, HEAD_DIM, HEAD_DIM), state_hgrn.dtype)
    zero_c = jnp.zeros((depth, b, state_conv.shape[2], f), state_conv.dtype)

    def attend_prompt(q, k, v, bias):
        return _sb_prompt(q, k, v, bias)

    def attend_sample(q, k, v, bias):
        return _sb_sample(q, k, v, bias, cache_k, cache_v, page_table)

    y_p, k_p, v_p, h_p, c_p = _run_trunk(x_prompt, p_prompt, zero_h, zero_c, attend_prompt, w,
                                         hgrn_bb=1, hgrn_hp=4, hgrn_tc=512, fuse_conv=True)
    y_s, k_s, v_s, h_s, c_s = _run_trunk(x_sample, p_sample, state_hgrn, state_conv, attend_sample, w,
                                         hgrn_bb=8, hgrn_hp=1, hgrn_tc=x_sample.shape[1], fuse_conv=False)
    hs = (heads, HEAD_DIM)
    return (y_p, y_s,
            k_p.reshape(k_p.shape[:2] + hs), v_p.reshape(v_p.shape[:2] + hs),
            k_s.reshape(k_s.shape[:2] + hs), v_s.reshape(v_s.shape[:2] + hs),
            h_p, h_s, c_p, c_s)
```

```python
import functools

import jax
import jax.numpy as jnp
from jax import lax
from jax.experimental import pallas as pl
from jax.experimental.pallas import tpu as pltpu

F32 = jnp.float32
BF16 = jnp.bfloat16
EPS = 1e-6
LOG2E = 1.4426950408889634
HEAD_DIM = 128
SUBLANES = 8
HGRN_CHUNK = 64
HGRN_SUB = 16
SB_BLOCK_Q = 512
SB_BLOCK_K = 512
SB_CUMSUM_BLOCK = 512
SB_PAGES_PER_STEP = 4
SB_HEADS_PER_STEP = 2
V7X_VMEM_CAP = 56 * 1024 * 1024
VMEM_TEMP_MARGIN = 16 * 1024 * 1024

NT_DIMS = (((1,), (1,)), ((), ()))
TN_DIMS = (((0,), (0,)), ((), ()))


def _params(semantics, block_bytes, scratch_bytes=0):
    limit = min(2 * block_bytes + scratch_bytes + VMEM_TEMP_MARGIN, V7X_VMEM_CAP)
    return pltpu.CompilerParams(dimension_semantics=semantics, vmem_limit_bytes=limit)


def _nbytes(shape, dtype):
    n = jnp.dtype(dtype).itemsize
    for s in shape:
        n *= s
    return n


def _rms(x, g):
    return x * lax.rsqrt(jnp.mean(x * x, axis=-1, keepdims=True) + EPS) * g


def _silu(x):
    return x * jax.nn.sigmoid(x)


def _softplus(z):
    return jnp.maximum(z, 0.0) + jnp.log(1.0 + jnp.exp(-jnp.abs(z)))


def _softplus2(z2):
    return jnp.maximum(z2, 0.0) + jnp.log2(1.0 + jnp.exp2(-jnp.abs(z2)))


def _bf16_split(x):
    hi = x.astype(BF16)
    lo = (x - hi.astype(F32)).astype(BF16)
    return hi, lo


def _dot(a, b, dims=None):
    if dims is None:
        return jnp.dot(a, b, preferred_element_type=F32)
    return lax.dot_general(a, b, dims, preferred_element_type=F32)


def _norm_matmul_kernel(x_ref, g_ref, w_ref, *rest, head_norm):
    if head_norm:
        hg_ref, o_ref, xn_ref = rest
    else:
        o_ref, xn_ref = rest

    @pl.when(pl.program_id(1) == 0)
    def _():
        xn_ref[...] = _rms(x_ref[...], g_ref[...]).astype(BF16)

    acc = _dot(xn_ref[...], w_ref[...])
    if head_norm:
        for c in range(acc.shape[1] // HEAD_DIM):
            cols = slice(c * HEAD_DIM, (c + 1) * HEAD_DIM)
            o_ref[:, cols] = _rms(acc[:, cols], hg_ref[...])
    else:
        o_ref[...] = acc


def _norm_matmul(x, g, w, *, n_out, head_g=None, tm=1024, tn=1024):
    m, k = x.shape
    tm, tn = min(tm, m), min(tn, n_out)
    assert m % tm == 0 and n_out % tn == 0 and w.shape[1] == n_out
    in_specs = [
        pl.BlockSpec((tm, k), lambda i, j: (i, 0)),
        pl.BlockSpec((1, k), lambda i, j: (0, 0)),
        pl.BlockSpec((k, tn), lambda i, j: (0, j)),
    ]
    args = [x, g.reshape(1, k), w]
    if head_g is not None:
        in_specs.append(pl.BlockSpec((1, HEAD_DIM), lambda i, j: (0, 0)))
        args.append(head_g.reshape(1, HEAD_DIM))
    blocks = _nbytes((tm, k), F32) + _nbytes((k, tn), BF16) + _nbytes((tm, tn), F32)
    return pl.pallas_call(
        functools.partial(_norm_matmul_kernel, head_norm=head_g is not None),
        grid=(m // tm, n_out // tn),
        in_specs=in_specs,
        out_specs=pl.BlockSpec((tm, tn), lambda i, j: (i, j)),
        out_shape=jax.ShapeDtypeStruct((m, n_out), F32),
        scratch_shapes=[pltpu.VMEM((tm, k), BF16)],
        compiler_params=_params(("parallel", "arbitrary"), blocks, _nbytes((tm, k), BF16)),
        name="norm_matmul_headnorm" if head_g is not None else "norm_matmul",
    )(*args)


def _kv_proj_kernel(x_ref, g_ref, wk_ref, wv_ref, hg_ref, k_ref, v_ref, xn_ref):
    @pl.when(pl.program_id(1) == 0)
    def _():
        xn_ref[...] = _rms(x_ref[...], g_ref[...]).astype(BF16)

    xn = xn_ref[...]
    acc = _dot(xn, wk_ref[...])
    for c in range(acc.shape[1] // HEAD_DIM):
        cols = slice(c * HEAD_DIM, (c + 1) * HEAD_DIM)
        k_ref[:, cols] = _rms(acc[:, cols], hg_ref[...])
    v_ref[...] = _dot(xn, wv_ref[...])


def _kv_proj(x, g, w, head_g, *, tm=1024, tn=512):
    m, d = x.shape
    n = w.shape[1] // 2
    tm, tn = min(tm, m), min(tn, n)
    assert m % tm == 0 and n % tn == 0 and tn % HEAD_DIM == 0
    nj = n // tn
    out_spec = pl.BlockSpec((tm, tn), lambda i, j: (i, j))
    blocks = _nbytes((tm, d), F32) + 2 * _nbytes((d, tn), BF16) + 2 * _nbytes((tm, tn), F32)
    return pl.pallas_call(
        _kv_proj_kernel,
        grid=(m // tm, nj),
        in_specs=[
            pl.BlockSpec((tm, d), lambda i, j: (i, 0)),
            pl.BlockSpec((1, d), lambda i, j: (0, 0)),
            pl.BlockSpec((d, tn), lambda i, j: (0, j)),
            pl.BlockSpec((d, tn), lambda i, j: (0, j + nj)),
            pl.BlockSpec((1, HEAD_DIM), lambda i, j: (0, 0)),
        ],
        out_specs=[out_spec, out_spec],
        out_shape=[jax.ShapeDtypeStruct((m, n), F32)] * 2,
        scratch_shapes=[pltpu.VMEM((tm, d), BF16)],
        compiler_params=_params(("parallel", "arbitrary"), blocks, _nbytes((tm, d), BF16)),
        name="kv_proj",
    )(x, g.reshape(1, d), w, w, head_g.reshape(1, HEAD_DIM))


def _ffn_in_kernel(x_ref, g_ref, wg_ref, wu_ref, buf_ref, cw_ref, cb_ref, act_ref, st_ref, xn_ref, tail_ref,
                   *, tiles_per_seq):
    i, j = pl.program_id(0), pl.program_id(1)
    tm = x_ref.shape[0]

    @pl.when(j == 0)
    def _():
        xn_ref[...] = _rms(x_ref[...], g_ref[...]).astype(BF16)

    @pl.when(i % tiles_per_seq == 0)
    def _():
        tail_ref[j, SUBLANES - 2:SUBLANES, :] = buf_ref[...]

    xn = xn_ref[...]
    ug = _dot(xn, wg_ref[...])
    t = lax.broadcasted_iota(jnp.int32, ug.shape, 0)
    h1 = tail_ref[j, SUBLANES - 1:SUBLANES, :]
    h2 = tail_ref[j, SUBLANES - 2:SUBLANES - 1, :]
    xm1 = jnp.where(t == 0, h1, pltpu.roll(ug, 1, axis=0))
    xm2 = jnp.where(t == 0, h2, jnp.where(t == 1, h1, pltpu.roll(ug, 2, axis=0)))
    conv = cb_ref[...] + xm2 * cw_ref[0:1, :] + xm1 * cw_ref[1:2, :] + ug * cw_ref[2:3, :]
    act_ref[...] = (_silu(conv) * _dot(xn, wu_ref[...])).astype(act_ref.dtype)
    tail_ref[j] = ug[tm - SUBLANES:tm, :]
    st_ref[...] = ug[tm - 2:tm, :]


def _ffn_in(x, g, w, buf, conv_w, conv_b, *, seq_len, tm=1024, tn=512):
    m, d = x.shape
    f = w.shape[1] // 2
    tm = min(tm, seq_len)
    assert seq_len % tm == 0 and m % seq_len == 0 and f % tn == 0 and buf.shape[1] == 2 and tm >= SUBLANES
    tiles_per_seq = seq_len // tm
    nj = f // tn
    blocks = (_nbytes((tm, d), F32) + 2 * _nbytes((d, tn), BF16) + _nbytes((tm, tn), BF16)
              + 2 * _nbytes((2, tn), F32))
    scratch_bytes = _nbytes((tm, d), BF16) + _nbytes((nj, SUBLANES, tn), F32)
    act, tails = pl.pallas_call(
        functools.partial(_ffn_in_kernel, tiles_per_seq=tiles_per_seq),
        grid=(m // tm, nj),
        in_specs=[
            pl.BlockSpec((tm, d), lambda i, j: (i, 0)),
            pl.BlockSpec((1, d), lambda i, j: (0, 0)),
            pl.BlockSpec((d, tn), lambda i, j: (0, j)),
            pl.BlockSpec((d, tn), lambda i, j: (0, j + nj)),
            pl.BlockSpec((None, 2, tn), lambda i, j: (i // tiles_per_seq, 0, j)),
            pl.BlockSpec((3, tn), lambda i, j: (0, j)),
            pl.BlockSpec((1, tn), lambda i, j: (0, j)),
        ],
        out_specs=[
            pl.BlockSpec((tm, tn), lambda i, j: (i, j)),
            pl.BlockSpec((None, 2, tn), lambda i, j: (i, 0, j)),
        ],
        out_shape=[
            jax.ShapeDtypeStruct((m, f), BF16),
            jax.ShapeDtypeStruct((m // tm, 2, f), F32),
        ],
        scratch_shapes=[pltpu.VMEM((tm, d), BF16), pltpu.VMEM((nj, SUBLANES, tn), F32)],
        compiler_params=_params(("arbitrary", "arbitrary"), blocks, scratch_bytes),
        name="ffn_in",
    )(x, g.reshape(1, d), w, w, buf, conv_w, conv_b.reshape(1, f))
    return act, tails[tiles_per_seq - 1::tiles_per_seq]


def _matmul_res_kernel(a_ref, w_ref, r_ref, o_ref):
    o_ref[...] = r_ref[...] + _dot(a_ref[...].astype(BF16), w_ref[...])


def _matmul_res(a, w, res, *, tm=1024, tn=512):
    m, k = a.shape
    n = w.shape[1]
    tm, tn = min(tm, m), min(tn, n)
    assert m % tm == 0 and n % tn == 0
    blocks = _nbytes((tm, k), a.dtype) + _nbytes((k, tn), BF16) + 2 * _nbytes((tm, tn), F32)
    return pl.pallas_call(
        _matmul_res_kernel,
        grid=(m // tm, n // tn),
        in_specs=[
            pl.BlockSpec((tm, k), lambda i, j: (i, 0)),
            pl.BlockSpec((k, tn), lambda i, j: (0, j)),
            pl.BlockSpec((tm, tn), lambda i, j: (i, j)),
        ],
        out_specs=pl.BlockSpec((tm, tn), lambda i, j: (i, j)),
        out_shape=jax.ShapeDtypeStruct((m, n), F32),
        compiler_params=_params(("parallel", "arbitrary"), blocks),
        name="matmul_res",
    )(a, w, res)


def _ple_kernel(x_ref, p_ref, g_ref, wpe_ref, wg_ref, o_ref, xn_ref):
    j = pl.program_id(1)

    @pl.when(j == 0)
    def _():
        xn_ref[...] = _rms(x_ref[...], g_ref[...]).astype(BF16)

    tn = o_ref.shape[1]
    gate = jax.nn.sigmoid(_dot(xn_ref[...], wg_ref[...]))
    pe = _dot(p_ref[...].astype(BF16), wpe_ref[...])
    o_ref[...] = x_ref[:, pl.ds(pl.multiple_of(j * tn, tn), tn)] + pe * gate


def _ple(x, p, g, w_pe, w_gate, *, tm=512, tn=1024):
    m, d = x.shape
    pd = p.shape[1]
    tm, tn = min(tm, m), min(tn, d)
    assert m % tm == 0 and d % tn == 0
    blocks = (_nbytes((tm, d), F32) + _nbytes((tm, tn), F32) + _nbytes((tm, pd), F32)
              + _nbytes((pd, tn), BF16) + _nbytes((d, tn), BF16))
    return pl.pallas_call(
        _ple_kernel,
        grid=(m // tm, d // tn),
        in_specs=[
            pl.BlockSpec((tm, d), lambda i, j: (i, 0)),
            pl.BlockSpec((tm, pd), lambda i, j: (i, 0)),
            pl.BlockSpec((1, d), lambda i, j: (0, 0)),
            pl.BlockSpec((pd, tn), lambda i, j: (0, j)),
            pl.BlockSpec((d, tn), lambda i, j: (0, j)),
        ],
        out_specs=pl.BlockSpec((tm, tn), lambda i, j: (i, j)),
        out_shape=jax.ShapeDtypeStruct((m, d), F32),
        scratch_shapes=[pltpu.VMEM((tm, d), BF16)],
        compiler_params=_params(("parallel", "arbitrary"), blocks, _nbytes((tm, d), BF16)),
        name="ple",
    )(x, p, g.reshape(1, d), w_pe, w_gate)


def _conv_gate_kernel(ug_ref, uu_ref, buf_ref, cw_ref, cb_ref, act_ref, st_ref):
    x = ug_ref[...]
    t_len = x.shape[1]
    t = lax.broadcasted_iota(jnp.int32, x.shape, 1)
    buf0 = buf_ref[:, 0:1, :]
    buf1 = buf_ref[:, 1:2, :]
    xm1 = jnp.where(t == 0, buf1, pltpu.roll(x, 1, axis=1))
    xm2 = jnp.where(t == 0, buf0, jnp.where(t == 1, buf1, pltpu.roll(x, 2, axis=1)))
    conv = cb_ref[...] + xm2 * cw_ref[0:1, :] + xm1 * cw_ref[1:2, :] + x * cw_ref[2:3, :]
    act_ref[...] = (_silu(conv) * uu_ref[...]).astype(act_ref.dtype)
    st_ref[...] = ug_ref[:, t_len - 2:t_len, :]


def _conv_gate(u, buf, conv_w, conv_b, *, bb, tk=512):
    b, t, f2 = u.shape
    f = f2 // 2
    assert b % bb == 0 and f % tk == 0 and t >= 2 and buf.shape[1] == 2
    nk = f // tk
    blocks = 2 * _nbytes((bb, t, tk), F32) + _nbytes((bb, t, tk), BF16) + 2 * _nbytes((bb, 2, tk), F32)
    return pl.pallas_call(
        _conv_gate_kernel,
        grid=(b // bb, nk),
        in_specs=[
            pl.BlockSpec((bb, t, tk), lambda i, j: (i, 0, j)),
            pl.BlockSpec((bb, t, tk), lambda i, j: (i, 0, j + nk)),
            pl.BlockSpec((bb, 2, tk), lambda i, j: (i, 0, j)),
            pl.BlockSpec((3, tk), lambda i, j: (0, j)),
            pl.BlockSpec((1, tk), lambda i, j: (0, j)),
        ],
        out_specs=[
            pl.BlockSpec((bb, t, tk), lambda i, j: (i, 0, j)),
            pl.BlockSpec((bb, 2, tk), lambda i, j: (i, 0, j)),
        ],
        out_shape=[
            jax.ShapeDtypeStruct((b, t, f), BF16),
            jax.ShapeDtypeStruct((b, 2, f), F32),
        ],
        compiler_params=_params(("parallel", "arbitrary"), blocks),
        name="conv_gate",
    )(u, u, buf, conv_w, conv_b.reshape(1, f))


def _cumsum_rows(x):
    n = x.shape[0]
    row = lax.broadcasted_iota(jnp.int32, x.shape, 0)
    s = 1
    while s < n:
        x = x + jnp.where(row >= s, pltpu.roll(x, s, axis=0), 0.0)
        s *= 2
    return x


def _hgrn_chunk(qz, fz, iv, state, lb, sub):
    c = qz.shape[0]
    f = lb + (1.0 - lb) * jax.nn.sigmoid(fz)
    q = _silu(qz)
    k = 1.0 - f
    b = _cumsum_rows(jnp.log(f))
    b_last = b[c - 1:c, :]
    o = _dot((q * jnp.exp(b)).astype(BF16), state.astype(BF16))

    lane = lax.broadcasted_iota(jnp.int32, (sub, c), 1)
    rowi = lax.broadcasted_iota(jnp.int32, (sub, c), 0)
    att_rows = []
    for i in range(c // sub):
        r0 = i * sub
        bi, qi, ki = b[r0:r0 + sub], q[r0:r0 + sub], k[r0:r0 + sub]
        att = jnp.zeros((sub, c), F32)
        for s in range(sub):
            pair = qi * ki[s:s + 1] * jnp.exp(jnp.minimum(bi - bi[s:s + 1], 0.0))
            col = jnp.sum(pair, axis=1, keepdims=True)
            att = jnp.where((lane == r0 + s) & (rowi >= s), col, att)
        if i > 0:
            mid = bi[0:1]
            qt = qi * jnp.exp(bi - mid)
            kt = k[:r0] * jnp.exp(mid - b[:r0])
            kt = jnp.concatenate([kt, jnp.zeros((c - r0, HEAD_DIM), F32)], axis=0)
            att = att + _dot(qt.astype(BF16), kt.astype(BF16), NT_DIMS)
        att_rows.append(att)
    att = att_rows[0] if len(att_rows) == 1 else jnp.concatenate(att_rows, axis=0)
    o = o + _dot(att.astype(BF16), iv.astype(BF16))

    decay = jnp.transpose(jnp.broadcast_to(jnp.exp(b_last), (HEAD_DIM, HEAD_DIM)))
    k_st = k * jnp.exp(b_last - b)
    new_state = decay * state + _dot(k_st.astype(BF16), iv.astype(BF16), TN_DIMS)
    return o, new_state


def _hgrn_kernel(q_ref, f_ref, i_ref, g_ref, alb_ref, gn_ref, s0_ref, og_ref, sout_ref, s_ref,
                 *, layer, chunk, sub):
    t_step = pl.program_id(2)

    @pl.when(t_step == 0)
    def _():
        s_ref[...] = s0_ref[...]

    a = alb_ref[...]
    e = jnp.exp(a - jnp.max(a, axis=0, keepdims=True))
    sm = e / jnp.sum(e, axis=0, keepdims=True)
    lb = jnp.sum(sm[:layer + 1], axis=0, keepdims=True)

    bb, tc, width = q_ref.shape

    def chunk_body(ci, carry):
        rows = pl.ds(pl.multiple_of(ci * chunk, chunk), chunk)
        for bi in range(bb):
            for hh in range(width // HEAD_DIM):
                cols = slice(hh * HEAD_DIM, (hh + 1) * HEAD_DIM)
                o, state = _hgrn_chunk(q_ref[bi, rows, cols], f_ref[bi, rows, cols], i_ref[bi, rows, cols],
                                       s_ref[bi, hh], lb[:, cols], sub)
                og_ref[bi, rows, cols] = _rms(o, gn_ref[...]) * _silu(g_ref[bi, rows, cols])
                s_ref[bi, hh] = state
        return carry

    lax.fori_loop(0, tc // chunk, chunk_body, 0)

    @pl.when(t_step == pl.num_programs(2) - 1)
    def _():
        sout_ref[...] = s_ref[...]


def _hgrn(proj, a_lb, g_norm, s0, *, layer, bb, hp, tc):
    b, t, d4 = proj.shape
    d = d4 // 4
    h = d // HEAD_DIM
    chunk = min(HGRN_CHUNK, t)
    sub = min(HGRN_SUB, chunk)
    assert b % bb == 0 and h % hp == 0 and t % tc == 0 and tc % chunk == 0 and chunk % sub == 0
    nl = a_lb.shape[0]
    width = hp * HEAD_DIM
    hg = h // hp

    def col(off):
        return pl.BlockSpec((bb, tc, width), lambda bi, hi, ti: (bi, ti, hi + off * hg))

    state_shape = (bb, hp, HEAD_DIM, HEAD_DIM)
    state_spec = pl.BlockSpec(state_shape, lambda bi, hi, ti: (bi, hi, 0, 0))
    blocks = 5 * _nbytes((bb, tc, width), F32) + 2 * _nbytes(state_shape, F32)
    return pl.pallas_call(
        functools.partial(_hgrn_kernel, layer=layer, chunk=chunk, sub=sub),
        grid=(b // bb, hg, t // tc),
        in_specs=[
            col(0), col(1), col(2), col(3),
            pl.BlockSpec((nl, width), lambda bi, hi, ti: (0, hi)),
            pl.BlockSpec((1, HEAD_DIM), lambda bi, hi, ti: (0, 0)),
            state_spec,
        ],
        out_specs=[
            pl.BlockSpec((bb, tc, width), lambda bi, hi, ti: (bi, ti, hi)),
            state_spec,
        ],
        out_shape=[
            jax.ShapeDtypeStruct((b, t, d), F32),
            jax.ShapeDtypeStruct(s0.shape, F32),
        ],
        scratch_shapes=[pltpu.VMEM(state_shape, F32)],
        compiler_params=_params(("parallel", "parallel", "arbitrary"), blocks, _nbytes(state_shape, F32)),
        name="hgrn",
    )(proj, proj, proj, proj, a_lb, g_norm.reshape(1, HEAD_DIM), s0)


def _sb_prompt_kernel(bias_ref, q_ref, k_ref, v_ref, o_ref, *, bq, bk, scale):
    group = pl.program_id(1)
    qi = pl.program_id(2)
    hp = q_ref.shape[1] // HEAD_DIM
    nsub = bq // bk
    heads = [slice(hh * HEAD_DIM, (hh + 1) * HEAD_DIM) for hh in range(hp)]
    bias = [bias_ref[group * hp + hh] * LOG2E for hh in range(hp)]
    q = [(q_ref[:, cols] * (scale * LOG2E)).astype(BF16) for cols in heads]
    cw = min(SB_CUMSUM_BLOCK, bk)
    row = lax.broadcasted_iota(jnp.int32, (cw, cw), 0)
    colm = lax.broadcasted_iota(jnp.int32, (cw, cw), 1)
    suffix = (row >= colm).astype(BF16)
    q_pos = lax.broadcasted_iota(jnp.int32, (bq, bk), 0)
    k_pos = lax.broadcasted_iota(jnp.int32, (bq, bk), 1)

    def visit(kb, carry, mask):
        rows = pl.ds(pl.multiple_of(kb * bk, bk), bk)
        out = []
        for hh, cols in enumerate(heads):
            acc, r = carry[hh]
            k = k_ref[rows, cols].astype(BF16)
            v = v_ref[rows, cols].astype(BF16)
            z = _dot(q[hh], k, NT_DIMS) + bias[hh]
            log_rest = -_softplus2(z)
            if mask is not None:
                log_rest = jnp.where(mask, log_rest, 0.0)
            lr = log_rest.astype(BF16)
            parts, newer = [], r
            for cb in reversed(range(bk // cw)):
                kcols = slice(cb * cw, (cb + 1) * cw)
                csum = _dot(lr[:, kcols], suffix) + newer
                parts.insert(0, csum)
                newer = csum[:, 0:1]
            a = jnp.exp2(z + (parts[0] if len(parts) == 1 else jnp.concatenate(parts, axis=1)))
            if mask is not None:
                a = jnp.where(mask, a, 0.0)
            out.append((acc + _dot(a.astype(BF16), v), newer))
        return tuple(out)

    carry = ((jnp.zeros((bq, HEAD_DIM), F32), jnp.zeros((bq, 1), F32)),) * hp
    for sub in reversed(range(nsub)):
        carry = visit(qi * nsub + sub, carry, k_pos + sub * bk < q_pos)
    carry = lax.fori_loop(0, qi * nsub, lambda step, c: visit(qi * nsub - 1 - step, c, None), carry)
    for hh, cols in enumerate(heads):
        o_ref[:, cols] = carry[hh][0].astype(o_ref.dtype)


def _sb_prompt(q, k, v, bias):
    b, t, d = q.shape
    bq, bk = min(SB_BLOCK_Q, t), min(SB_BLOCK_K, t)
    width = SB_HEADS_PER_STEP * HEAD_DIM
    assert t % bq == 0 and bq % bk == 0 and d % width == 0
    kv_spec = pl.BlockSpec((None, t, width), lambda bi, hi, qi: (bi, 0, hi))
    q_spec = pl.BlockSpec((None, bq, width), lambda bi, hi, qi: (bi, qi, hi))
    blocks = 2 * _nbytes((t, width), F32) + _nbytes((bq, width), F32) + _nbytes((bq, width), BF16)
    return pl.pallas_call(
        functools.partial(_sb_prompt_kernel, bq=bq, bk=bk, scale=HEAD_DIM ** -0.5),
        grid=(b, d // width, t // bq),
        in_specs=[pl.BlockSpec(memory_space=pltpu.SMEM), q_spec, kv_spec, kv_spec],
        out_specs=q_spec,
        out_shape=jax.ShapeDtypeStruct((b, t, d), BF16),
        compiler_params=_params(("parallel", "parallel", "arbitrary"), blocks),
        name="sb_prompt",
    )(bias, q, k, v)


def _slab_head(slab_ref, hh):
    keys, nh, dim = slab_ref.shape
    return slab_ref.reshape(keys * nh, dim)[pl.ds(hh, keys, stride=nh), :]


def _sb_sample_kernel(pt_ref, q_ref, kn_ref, vn_ref, bias_ref, *rest, pages, scale):
    nq, d = q_ref.shape
    h = d // HEAD_DIM
    groups = h // SUBLANES
    n_slabs = pages * groups
    k_refs, v_refs = rest[:n_slabs], rest[n_slabs:2 * n_slabs]
    o_ref, qbd_ref, kc_ref, vc_ref, acc_ref, r_ref = rest[2 * n_slabs:]
    step = pl.program_id(1)
    page = k_refs[0].shape[0]
    lanes = h * nq
    heads = [slice(hh * HEAD_DIM, (hh + 1) * HEAD_DIM) for hh in range(h)]

    def visit(n, mask):
        z = _dot(kc_ref[0:n, :], qbd_ref[...], NT_DIMS) + bias_ref[...]
        log_rest = -_softplus(z)
        if mask is not None:
            log_rest = jnp.where(mask, log_rest, 0.0)
        ri = lax.broadcasted_iota(jnp.int32, (n, n), 0)
        ci = lax.broadcasted_iota(jnp.int32, (n, n), 1)
        suffix = (ci >= ri).astype(BF16)
        hi, lo = _bf16_split(log_rest)
        csum = _dot(suffix, hi) + _dot(suffix, lo)
        a = jnp.exp(z + csum + r_ref[...])
        if mask is not None:
            a = jnp.where(mask, a, 0.0)
        acc_ref[...] += _dot(jnp.transpose(a).astype(BF16), vc_ref[0:n, :])
        r_ref[...] += csum[0:1, :]

    @pl.when(step == 0)
    def _():
        qs = q_ref[...] * scale
        qt = jnp.broadcast_to(qs[None], (h, nq, d)).reshape(lanes, d)
        rr = lax.broadcasted_iota(jnp.int32, (lanes, d), 0)
        cc = lax.broadcasted_iota(jnp.int32, (lanes, d), 1)
        qbd_ref[...] = jnp.where(rr // nq == cc // HEAD_DIM, qt, 0.0).astype(BF16)
        acc_ref[...] = jnp.zeros_like(acc_ref)
        r_ref[...] = jnp.zeros_like(r_ref)
        pad = jnp.zeros((page - nq, d), BF16)
        kc_ref[0:page, :] = jnp.concatenate([kn_ref[...].astype(BF16), pad], axis=0)
        vc_ref[0:page, :] = jnp.concatenate([vn_ref[...].astype(BF16), pad], axis=0)
        key = lax.broadcasted_iota(jnp.int32, (page, lanes), 0)
        query = lax.broadcasted_iota(jnp.int32, (page, lanes), 1) % nq
        visit(page, key < query)

    @pl.when(step > 0)
    def _():
        for p in range(pages):
            rows = slice((pages - 1 - p) * page, (pages - p) * page)
            for g in range(groups):
                for hh in range(SUBLANES):
                    cols = heads[g * SUBLANES + hh]
                    kc_ref[rows, cols] = _slab_head(k_refs[p * groups + g], hh).astype(BF16)
                    vc_ref[rows, cols] = _slab_head(v_refs[p * groups + g], hh).astype(BF16)
        visit(pages * page, None)

    @pl.when(step == pl.num_programs(1) - 1)
    def _():
        for hh, cols in enumerate(heads):
            o_ref[:, cols] = acc_ref[hh * nq:(hh + 1) * nq, cols]


def _sb_sample(q, k_new, v_new, bias, cache_k, cache_v, page_table):
    b, nq, d = q.shape
    n_pool, page, h, dh = cache_k.shape
    n_pages = page_table.shape[1]
    pages = min(SB_PAGES_PER_STEP, n_pages)
    assert n_pages % pages == 0 and h * nq == HEAD_DIM and nq <= page and dh == HEAD_DIM and h * dh == d and h % SUBLANES == 0
    lanes = h * nq

    def slab_spec(p, g):
        def index(bi, si, pt):
            group = jnp.maximum(si - 1, 0)
            return (pt[bi * n_pages + n_pages - 1 - (group * pages + p)], 0, g, 0)
        return pl.BlockSpec((None, page, SUBLANES, dh), index)

    slab_specs = [slab_spec(p, g) for p in range(pages) for g in range(h // SUBLANES)]

    tok_spec = pl.BlockSpec((None, nq, d), lambda bi, si, pt: (bi, 0, 0))
    scratch = [
        pltpu.VMEM((lanes, d), BF16),
        pltpu.VMEM((pages * page, d), BF16),
        pltpu.VMEM((pages * page, d), BF16),
        pltpu.VMEM((lanes, d), F32),
        pltpu.VMEM((1, lanes), F32),
    ]
    blocks = 2 * pages * _nbytes((page, d), F32) + 4 * _nbytes((nq, d), F32)
    scratch_bytes = (_nbytes((lanes, d), BF16) + 2 * _nbytes((pages * page, d), BF16)
                     + _nbytes((lanes, d), F32))
    grid_spec = pltpu.PrefetchScalarGridSpec(
        num_scalar_prefetch=1,
        grid=(b, 1 + n_pages // pages),
        in_specs=[tok_spec, tok_spec, tok_spec,
                  pl.BlockSpec((1, lanes), lambda bi, si, pt: (0, 0))]
                 + slab_specs * 2,
        out_specs=tok_spec,
        scratch_shapes=scratch,
    )
    return pl.pallas_call(
        functools.partial(_sb_sample_kernel, pages=pages, scale=HEAD_DIM ** -0.5),
        grid_spec=grid_spec,
        out_shape=jax.ShapeDtypeStruct((b, nq, d), F32),
        compiler_params=_params(("parallel", "arbitrary"), blocks, scratch_bytes),
        name="sb_sample",
    )(page_table.reshape(-1), q, k_new, v_new, jnp.repeat(bias, nq).reshape(1, lanes),
      *([cache_k] * len(slab_specs)), *([cache_v] * len(slab_specs)))


def _run_trunk(x3, p, hgrn_state, conv_state, attend, w, *, hgrn_bb, hgrn_hp, hgrn_tc, fuse_conv):
    b, t, d = x3.shape
    m = b * t
    depth = w["norm_mix"].shape[0]
    n_a = w["a_w_in"].shape[0]
    f = w["ffn_w_out"].shape[1]
    x = x3.reshape(m, d)
    new_s, new_c = [], []
    k_sh = v_sh = None
    for layer in range(depth):
        if layer < n_a:
            proj = _norm_matmul(x, w["norm_mix"][layer], w["a_w_in"][layer], n_out=4 * d)
            og, s = _hgrn(proj.reshape(b, t, 4 * d), w["a_lb"], w["a_g_norm"][layer], hgrn_state[layer],
                          layer=layer, bb=hgrn_bb, hp=hgrn_hp, tc=hgrn_tc)
            new_s.append(s)
            x = _matmul_res(og.reshape(m, d), w["a_w_out"][layer], x, tn=1024)
        else:
            bi = layer - n_a
            q = _norm_matmul(x, w["norm_mix"][layer], w["b_w_q"][bi], n_out=d, head_g=w["b_q_norm"][bi])
            o = attend(q.reshape(b, t, d), k_sh, v_sh, w["b_sb_bias"][bi])
            x = _matmul_res(o.reshape(m, d), w["b_w_out"][bi], x, tn=1024)
        if fuse_conv:
            act, cbuf = _ffn_in(x, w["norm_ffn"][layer], w["ffn_w_in"][layer], conv_state[layer],
                                w["ffn_conv_w"][layer], w["ffn_conv_b"][layer], seq_len=t)
        else:
            u = _norm_matmul(x, w["norm_ffn"][layer], w["ffn_w_in"][layer], n_out=2 * f)
            act, cbuf = _conv_gate(u.reshape(b, t, 2 * f), conv_state[layer], w["ffn_conv_w"][layer],
                                   w["ffn_conv_b"][layer], bb=b)
        new_c.append(cbuf)
        x = _matmul_res(act.reshape(m, f), w["ffn_w_out"][layer], x)
        x = _ple(x, p[layer].reshape(m, -1), w["ple_norm"][layer], w["ple_w"][layer], w["ple_w_gate"][layer])
        if layer == n_a - 1:
            k_sh, v_sh = _kv_proj(x, w["kv_norm"], w["kv_w"], w["k_norm"])
            k_sh, v_sh = k_sh.reshape(b, t, d), v_sh.reshape(b, t, d)
    return x.reshape(b, t, d), k_sh, v_sh, jnp.stack(new_s), jnp.stack(new_c)


def kernel(x_prompt, x_sample, p_prompt, p_sample, cache_k, cache_v, page_table, state_hgrn, state_conv, norm_mix, norm_ffn, a_w_in, a_lb, a_g_norm, a_w_out, kv_norm, kv_w, k_norm, b_w_q, b_q_norm, b_sb_bias, b_w_out, ffn_w_in, ffn_conv_w, ffn_conv_b, ffn_w_out, ple_w, ple_norm, ple_w_gate):
    w = dict(norm_mix=norm_mix, norm_ffn=norm_ffn, a_lb=a_lb, a_g_norm=a_g_norm, kv_norm=kv_norm,
             k_norm=k_norm, b_q_norm=b_q_norm, b_sb_bias=b_sb_bias, ffn_conv_w=ffn_conv_w,
             ffn_conv_b=ffn_conv_b, ple_norm=ple_norm)
    for name, val in dict(a_w_in=a_w_in, a_w_out=a_w_out, kv_w=kv_w, b_w_q=b_w_q, b_w_out=b_w_out,
                          ffn_w_in=ffn_w_in, ffn_w_out=ffn_w_out, ple_w=ple_w, ple_w_gate=ple_w_gate).items():
        w[name] = val.astype(BF16)

    b, t, d = x_prompt.shape
    heads = d // HEAD_DIM
    n_a = a_w_in.shape[0]
    depth = norm_mix.shape[0]
    f = ffn_w_out.shape[1]
    zero_h = jnp.zeros((n_a, b, heads, HEAD_DIM, HEAD_DIM), state_hgrn.dtype)
    zero_c = jnp.zeros((depth, b, state_conv.shape[2], f), state_conv.dtype)

    def attend_prompt(q, k, v, bias):
        return _sb_prompt(q, k, v, bias)

    def attend_sample(q, k, v, bias):
        return _sb_sample(q, k, v, bias, cache_k, cache_v, page_table)

    y_p, k_p, v_p, h_p, c_p = _run_trunk(x_prompt, p_prompt, zero_h, zero_c, attend_prompt, w,
                                         hgrn_bb=1, hgrn_hp=8, hgrn_tc=512, fuse_conv=True)
    y_s, k_s, v_s, h_s, c_s = _run_trunk(x_sample, p_sample, state_hgrn, state_conv, attend_sample, w,
                                         hgrn_bb=8, hgrn_hp=1, hgrn_tc=x_sample.shape[1], fuse_conv=False)
    hs = (heads, HEAD_DIM)
    return (y_p, y_s,
            k_p.reshape(k_p.shape[:2] + hs), v_p.reshape(v_p.shape[:2] + hs),
            k_s.reshape(k_s.shape[:2] + hs), v_s.reshape(v_s.shape[:2] + hs),
            h_p, h_s, c_p, c_s)
```

```python
import functools

import jax
import jax.numpy as jnp
from jax import lax
from jax.experimental import pallas as pl
from jax.experimental.pallas import tpu as pltpu

F32 = jnp.float32
BF16 = jnp.bfloat16
EPS = 1e-6
LOG2E = 1.4426950408889634
HEAD_DIM = 128
SUBLANES = 8
HGRN_CHUNK = 64
HGRN_SUB = 16
SB_BLOCK_Q = 512
SB_BLOCK_K = 512
SB_CUMSUM_BLOCK = 256
SB_PAGES_PER_STEP = 4
SB_HEADS_PER_STEP = 2
V7X_VMEM_CAP = 56 * 1024 * 1024
VMEM_TEMP_MARGIN = 16 * 1024 * 1024

NT_DIMS = (((1,), (1,)), ((), ()))
TN_DIMS = (((0,), (0,)), ((), ()))


def _params(semantics, block_bytes, scratch_bytes=0):
    limit = min(2 * block_bytes + scratch_bytes + VMEM_TEMP_MARGIN, V7X_VMEM_CAP)
    return pltpu.CompilerParams(dimension_semantics=semantics, vmem_limit_bytes=limit)


def _nbytes(shape, dtype):
    n = jnp.dtype(dtype).itemsize
    for s in shape:
        n *= s
    return n


def _rms(x, g):
    return x * lax.rsqrt(jnp.mean(x * x, axis=-1, keepdims=True) + EPS) * g


def _silu(x):
    return x * jax.nn.sigmoid(x)


def _softplus(z):
    return jnp.maximum(z, 0.0) + jnp.log(1.0 + jnp.exp(-jnp.abs(z)))


def _softplus2(z2):
    return jnp.maximum(z2, 0.0) + jnp.log2(1.0 + jnp.exp2(-jnp.abs(z2)))


def _bf16_split(x):
    hi = x.astype(BF16)
    lo = (x - hi.astype(F32)).astype(BF16)
    return hi, lo


def _dot(a, b, dims=None):
    if dims is None:
        return jnp.dot(a, b, preferred_element_type=F32)
    return lax.dot_general(a, b, dims, preferred_element_type=F32)


def _norm_matmul_kernel(x_ref, g_ref, w_ref, *rest, head_norm):
    if head_norm:
        hg_ref, o_ref, xn_ref = rest
    else:
        o_ref, xn_ref = rest

    @pl.when(pl.program_id(1) == 0)
    def _():
        xn_ref[...] = _rms(x_ref[...], g_ref[...]).astype(BF16)

    acc = _dot(xn_ref[...], w_ref[...])
    if head_norm:
        for c in range(acc.shape[1] // HEAD_DIM):
            cols = slice(c * HEAD_DIM, (c + 1) * HEAD_DIM)
            o_ref[:, cols] = _rms(acc[:, cols], hg_ref[...])
    else:
        o_ref[...] = acc


def _norm_matmul(x, g, w, *, n_out, head_g=None, tm=1024, tn=1024):
    m, k = x.shape
    tm, tn = min(tm, m), min(tn, n_out)
    assert m % tm == 0 and n_out % tn == 0 and w.shape[1] == n_out
    in_specs = [
        pl.BlockSpec((tm, k), lambda i, j: (i, 0)),
        pl.BlockSpec((1, k), lambda i, j: (0, 0)),
        pl.BlockSpec((k, tn), lambda i, j: (0, j)),
    ]
    args = [x, g.reshape(1, k), w]
    if head_g is not None:
        in_specs.append(pl.BlockSpec((1, HEAD_DIM), lambda i, j: (0, 0)))
        args.append(head_g.reshape(1, HEAD_DIM))
    blocks = _nbytes((tm, k), F32) + _nbytes((k, tn), BF16) + _nbytes((tm, tn), F32)
    return pl.pallas_call(
        functools.partial(_norm_matmul_kernel, head_norm=head_g is not None),
        grid=(m // tm, n_out // tn),
        in_specs=in_specs,
        out_specs=pl.BlockSpec((tm, tn), lambda i, j: (i, j)),
        out_shape=jax.ShapeDtypeStruct((m, n_out), F32),
        scratch_shapes=[pltpu.VMEM((tm, k), BF16)],
        compiler_params=_params(("parallel", "arbitrary"), blocks, _nbytes((tm, k), BF16)),
        name="norm_matmul_headnorm" if head_g is not None else "norm_matmul",
    )(*args)


def _kv_proj_kernel(x_ref, g_ref, wk_ref, wv_ref, hg_ref, k_ref, v_ref, xn_ref):
    @pl.when(pl.program_id(1) == 0)
    def _():
        xn_ref[...] = _rms(x_ref[...], g_ref[...]).astype(BF16)

    xn = xn_ref[...]
    acc = _dot(xn, wk_ref[...])
    for c in range(acc.shape[1] // HEAD_DIM):
        cols = slice(c * HEAD_DIM, (c + 1) * HEAD_DIM)
        k_ref[:, cols] = _rms(acc[:, cols], hg_ref[...])
    v_ref[...] = _dot(xn, wv_ref[...])


def _kv_proj(x, g, w, head_g, *, tm=1024, tn=512):
    m, d = x.shape
    n = w.shape[1] // 2
    tm, tn = min(tm, m), min(tn, n)
    assert m % tm == 0 and n % tn == 0 and tn % HEAD_DIM == 0
    nj = n // tn
    out_spec = pl.BlockSpec((tm, tn), lambda i, j: (i, j))
    blocks = _nbytes((tm, d), F32) + 2 * _nbytes((d, tn), BF16) + 2 * _nbytes((tm, tn), F32)
    return pl.pallas_call(
        _kv_proj_kernel,
        grid=(m // tm, nj),
        in_specs=[
            pl.BlockSpec((tm, d), lambda i, j: (i, 0)),
            pl.BlockSpec((1, d), lambda i, j: (0, 0)),
            pl.BlockSpec((d, tn), lambda i, j: (0, j)),
            pl.BlockSpec((d, tn), lambda i, j: (0, j + nj)),
            pl.BlockSpec((1, HEAD_DIM), lambda i, j: (0, 0)),
        ],
        out_specs=[out_spec, out_spec],
        out_shape=[jax.ShapeDtypeStruct((m, n), F32)] * 2,
        scratch_shapes=[pltpu.VMEM((tm, d), BF16)],
        compiler_params=_params(("parallel", "arbitrary"), blocks, _nbytes((tm, d), BF16)),
        name="kv_proj",
    )(x, g.reshape(1, d), w, w, head_g.reshape(1, HEAD_DIM))


def _ffn_in_kernel(x_ref, g_ref, wg_ref, wu_ref, buf_ref, cw_ref, cb_ref, act_ref, st_ref, xn_ref, tail_ref,
                   *, tiles_per_seq):
    i, j = pl.program_id(0), pl.program_id(1)
    tm = x_ref.shape[0]

    @pl.when(j == 0)
    def _():
        xn_ref[...] = _rms(x_ref[...], g_ref[...]).astype(BF16)

    @pl.when(i % tiles_per_seq == 0)
    def _():
        tail_ref[j, SUBLANES - 2:SUBLANES, :] = buf_ref[...]

    xn = xn_ref[...]
    ug = _dot(xn, wg_ref[...])
    t = lax.broadcasted_iota(jnp.int32, ug.shape, 0)
    h1 = tail_ref[j, SUBLANES - 1:SUBLANES, :]
    h2 = tail_ref[j, SUBLANES - 2:SUBLANES - 1, :]
    xm1 = jnp.where(t == 0, h1, pltpu.roll(ug, 1, axis=0))
    xm2 = jnp.where(t == 0, h2, jnp.where(t == 1, h1, pltpu.roll(ug, 2, axis=0)))
    conv = cb_ref[...] + xm2 * cw_ref[0:1, :] + xm1 * cw_ref[1:2, :] + ug * cw_ref[2:3, :]
    act_ref[...] = (_silu(conv) * _dot(xn, wu_ref[...])).astype(act_ref.dtype)
    tail_ref[j] = ug[tm - SUBLANES:tm, :]
    st_ref[...] = ug[tm - 2:tm, :]


def _ffn_in(x, g, w, buf, conv_w, conv_b, *, seq_len, tm=1024, tn=512):
    m, d = x.shape
    f = w.shape[1] // 2
    tm = min(tm, seq_len)
    assert seq_len % tm == 0 and m % seq_len == 0 and f % tn == 0 and buf.shape[1] == 2 and tm >= SUBLANES
    tiles_per_seq = seq_len // tm
    nj = f // tn
    blocks = (_nbytes((tm, d), F32) + 2 * _nbytes((d, tn), BF16) + _nbytes((tm, tn), BF16)
              + 2 * _nbytes((2, tn), F32))
    scratch_bytes = _nbytes((tm, d), BF16) + _nbytes((nj, SUBLANES, tn), F32)
    act, tails = pl.pallas_call(
        functools.partial(_ffn_in_kernel, tiles_per_seq=tiles_per_seq),
        grid=(m // tm, nj),
        in_specs=[
            pl.BlockSpec((tm, d), lambda i, j: (i, 0)),
            pl.BlockSpec((1, d), lambda i, j: (0, 0)),
            pl.BlockSpec((d, tn), lambda i, j: (0, j)),
            pl.BlockSpec((d, tn), lambda i, j: (0, j + nj)),
            pl.BlockSpec((None, 2, tn), lambda i, j: (i // tiles_per_seq, 0, j)),
            pl.BlockSpec((3, tn), lambda i, j: (0, j)),
            pl.BlockSpec((1, tn), lambda i, j: (0, j)),
        ],
        out_specs=[
            pl.BlockSpec((tm, tn), lambda i, j: (i, j)),
            pl.BlockSpec((None, 2, tn), lambda i, j: (i, 0, j)),
        ],
        out_shape=[
            jax.ShapeDtypeStruct((m, f), BF16),
            jax.ShapeDtypeStruct((m // tm, 2, f), F32),
        ],
        scratch_shapes=[pltpu.VMEM((tm, d), BF16), pltpu.VMEM((nj, SUBLANES, tn), F32)],
        compiler_params=_params(("arbitrary", "arbitrary"), blocks, scratch_bytes),
        name="ffn_in",
    )(x, g.reshape(1, d), w, w, buf, conv_w, conv_b.reshape(1, f))
    return act, tails[tiles_per_seq - 1::tiles_per_seq]


def _matmul_res_kernel(a_ref, w_ref, r_ref, o_ref):
    o_ref[...] = r_ref[...] + _dot(a_ref[...].astype(BF16), w_ref[...])


def _matmul_res(a, w, res, *, tm=1024, tn=512):
    m, k = a.shape
    n = w.shape[1]
    tm, tn = min(tm, m), min(tn, n)
    assert m % tm == 0 and n % tn == 0
    blocks = _nbytes((tm, k), a.dtype) + _nbytes((k, tn), BF16) + 2 * _nbytes((tm, tn), F32)
    return pl.pallas_call(
        _matmul_res_kernel,
        grid=(m // tm, n // tn),
        in_specs=[
            pl.BlockSpec((tm, k), lambda i, j: (i, 0)),
            pl.BlockSpec((k, tn), lambda i, j: (0, j)),
            pl.BlockSpec((tm, tn), lambda i, j: (i, j)),
        ],
        out_specs=pl.BlockSpec((tm, tn), lambda i, j: (i, j)),
        out_shape=jax.ShapeDtypeStruct((m, n), F32),
        compiler_params=_params(("parallel", "arbitrary"), blocks),
        name="matmul_res",
    )(a, w, res)


def _ple_kernel(x_ref, p_ref, g_ref, wpe_ref, wg_ref, o_ref, xn_ref):
    j = pl.program_id(1)

    @pl.when(j == 0)
    def _():
        xn_ref[...] = _rms(x_ref[...], g_ref[...]).astype(BF16)

    tn = o_ref.shape[1]
    gate = jax.nn.sigmoid(_dot(xn_ref[...], wg_ref[...]))
    pe = _dot(p_ref[...].astype(BF16), wpe_ref[...])
    o_ref[...] = x_ref[:, pl.ds(pl.multiple_of(j * tn, tn), tn)] + pe * gate


def _ple(x, p, g, w_pe, w_gate, *, tm=512, tn=1024):
    m, d = x.shape
    pd = p.shape[1]
    tm, tn = min(tm, m), min(tn, d)
    assert m % tm == 0 and d % tn == 0
    blocks = (_nbytes((tm, d), F32) + _nbytes((tm, tn), F32) + _nbytes((tm, pd), F32)
              + _nbytes((pd, tn), BF16) + _nbytes((d, tn), BF16))
    return pl.pallas_call(
        _ple_kernel,
        grid=(m // tm, d // tn),
        in_specs=[
            pl.BlockSpec((tm, d), lambda i, j: (i, 0)),
            pl.BlockSpec((tm, pd), lambda i, j: (i, 0)),
            pl.BlockSpec((1, d), lambda i, j: (0, 0)),
            pl.BlockSpec((pd, tn), lambda i, j: (0, j)),
            pl.BlockSpec((d, tn), lambda i, j: (0, j)),
        ],
        out_specs=pl.BlockSpec((tm, tn), lambda i, j: (i, j)),
        out_shape=jax.ShapeDtypeStruct((m, d), F32),
        scratch_shapes=[pltpu.VMEM((tm, d), BF16)],
        compiler_params=_params(("parallel", "arbitrary"), blocks, _nbytes((tm, d), BF16)),
        name="ple",
    )(x, p, g.reshape(1, d), w_pe, w_gate)


def _conv_gate_kernel(ug_ref, uu_ref, buf_ref, cw_ref, cb_ref, act_ref, st_ref):
    x = ug_ref[...]
    t_len = x.shape[1]
    t = lax.broadcasted_iota(jnp.int32, x.shape, 1)
    buf0 = buf_ref[:, 0:1, :]
    buf1 = buf_ref[:, 1:2, :]
    xm1 = jnp.where(t == 0, buf1, pltpu.roll(x, 1, axis=1))
    xm2 = jnp.where(t == 0, buf0, jnp.where(t == 1, buf1, pltpu.roll(x, 2, axis=1)))
    conv = cb_ref[...] + xm2 * cw_ref[0:1, :] + xm1 * cw_ref[1:2, :] + x * cw_ref[2:3, :]
    act_ref[...] = (_silu(conv) * uu_ref[...]).astype(act_ref.dtype)
    st_ref[...] = ug_ref[:, t_len - 2:t_len, :]


def _conv_gate(u, buf, conv_w, conv_b, *, bb, tk=512):
    b, t, f2 = u.shape
    f = f2 // 2
    assert b % bb == 0 and f % tk == 0 and t >= 2 and buf.shape[1] == 2
    nk = f // tk
    blocks = 2 * _nbytes((bb, t, tk), F32) + _nbytes((bb, t, tk), BF16) + 2 * _nbytes((bb, 2, tk), F32)
    return pl.pallas_call(
        _conv_gate_kernel,
        grid=(b // bb, nk),
        in_specs=[
            pl.BlockSpec((bb, t, tk), lambda i, j: (i, 0, j)),
            pl.BlockSpec((bb, t, tk), lambda i, j: (i, 0, j + nk)),
            pl.BlockSpec((bb, 2, tk), lambda i, j: (i, 0, j)),
            pl.BlockSpec((3, tk), lambda i, j: (0, j)),
            pl.BlockSpec((1, tk), lambda i, j: (0, j)),
        ],
        out_specs=[
            pl.BlockSpec((bb, t, tk), lambda i, j: (i, 0, j)),
            pl.BlockSpec((bb, 2, tk), lambda i, j: (i, 0, j)),
        ],
        out_shape=[
            jax.ShapeDtypeStruct((b, t, f), BF16),
            jax.ShapeDtypeStruct((b, 2, f), F32),
        ],
        compiler_params=_params(("parallel", "arbitrary"), blocks),
        name="conv_gate",
    )(u, u, buf, conv_w, conv_b.reshape(1, f))


def _cumsum_rows(x):
    n = x.shape[0]
    row = lax.broadcasted_iota(jnp.int32, x.shape, 0)
    s = 1
    while s < n:
        x = x + jnp.where(row >= s, pltpu.roll(x, s, axis=0), 0.0)
        s *= 2
    return x


def _hgrn_chunk(qz, fz, iv, state, lb, sub, state_t):
    c = qz.shape[0]
    f = lb + (1.0 - lb) * jax.nn.sigmoid(fz)
    q = _silu(qz)
    k = 1.0 - f
    b2 = _cumsum_rows(jnp.log2(f))
    b2_last = b2[c - 1:c, :]
    o = _dot((q * jnp.exp2(b2)).astype(BF16), state.astype(BF16), NT_DIMS if state_t else None)

    lane = lax.broadcasted_iota(jnp.int32, (sub, c), 1)
    rowi = lax.broadcasted_iota(jnp.int32, (sub, c), 0)
    att_rows = []
    for i in range(c // sub):
        r0 = i * sub
        bi, qi, ki = b2[r0:r0 + sub], q[r0:r0 + sub], k[r0:r0 + sub]
        att = jnp.zeros((sub, c), F32)
        for s in range(sub):
            pair = qi * ki[s:s + 1] * jnp.exp2(jnp.minimum(bi - bi[s:s + 1], 0.0))
            col = jnp.sum(pair, axis=1, keepdims=True)
            att = jnp.where((lane == r0 + s) & (rowi >= s), col, att)
        if i > 0:
            mid = bi[0:1]
            qt = qi * jnp.exp2(bi - mid)
            kt = k[:r0] * jnp.exp2(mid - b2[:r0])
            kt = jnp.concatenate([kt, jnp.zeros((c - r0, HEAD_DIM), F32)], axis=0)
            att = att + _dot(qt.astype(BF16), kt.astype(BF16), NT_DIMS)
        att_rows.append(att)
    att = att_rows[0] if len(att_rows) == 1 else jnp.concatenate(att_rows, axis=0)
    o = o + _dot(att.astype(BF16), iv.astype(BF16))

    k_st = (k * jnp.exp2(b2_last - b2)).astype(BF16)
    if state_t:
        new_state = jnp.exp2(b2_last) * state + _dot(iv.astype(BF16), k_st, TN_DIMS)
    else:
        decay = jnp.transpose(jnp.broadcast_to(jnp.exp2(b2_last), (HEAD_DIM, HEAD_DIM)))
        new_state = decay * state + _dot(k_st, iv.astype(BF16), TN_DIMS)
    return o, new_state


def _hgrn_kernel(q_ref, f_ref, i_ref, g_ref, alb_ref, gn_ref, s0_ref, og_ref, sout_ref, s_ref,
                 *, layer, chunk, sub):
    t_step = pl.program_id(2)
    bb, tc, width = q_ref.shape
    hp = width // HEAD_DIM
    state_t = pl.num_programs(2) * (tc // chunk) > 1

    def relayout(dst_ref, src_ref):
        for bi in range(bb):
            for hh in range(hp):
                dst_ref[bi, hh] = jnp.transpose(src_ref[bi, hh]) if state_t else src_ref[bi, hh]

    @pl.when(t_step == 0)
    def _():
        relayout(s_ref, s0_ref)

    a = alb_ref[...]
    e = jnp.exp(a - jnp.max(a, axis=0, keepdims=True))
    sm = e / jnp.sum(e, axis=0, keepdims=True)
    lb = jnp.sum(sm[:layer + 1], axis=0, keepdims=True)

    def chunk_body(ci, carry):
        rows = pl.ds(pl.multiple_of(ci * chunk, chunk), chunk)
        for bi in range(bb):
            for hh in range(hp):
                cols = slice(hh * HEAD_DIM, (hh + 1) * HEAD_DIM)
                o, state = _hgrn_chunk(q_ref[bi, rows, cols], f_ref[bi, rows, cols], i_ref[bi, rows, cols],
                                       s_ref[bi, hh], lb[:, cols], sub, state_t)
                og_ref[bi, rows, cols] = _rms(o, gn_ref[...]) * _silu(g_ref[bi, rows, cols])
                s_ref[bi, hh] = state
        return carry

    lax.fori_loop(0, tc // chunk, chunk_body, 0)

    @pl.when(t_step == pl.num_programs(2) - 1)
    def _():
        relayout(sout_ref, s_ref)


def _hgrn(proj, a_lb, g_norm, s0, *, layer, bb, hp, tc):
    b, t, d4 = proj.shape
    d = d4 // 4
    h = d // HEAD_DIM
    chunk = min(HGRN_CHUNK, t)
    sub = min(HGRN_SUB, chunk)
    assert b % bb == 0 and h % hp == 0 and t % tc == 0 and tc % chunk == 0 and chunk % sub == 0
    nl = a_lb.shape[0]
    width = hp * HEAD_DIM
    hg = h // hp

    def col(off):
        return pl.BlockSpec((bb, tc, width), lambda bi, hi, ti: (bi, ti, hi + off * hg))

    state_shape = (bb, hp, HEAD_DIM, HEAD_DIM)
    state_spec = pl.BlockSpec(state_shape, lambda bi, hi, ti: (bi, hi, 0, 0))
    blocks = 5 * _nbytes((bb, tc, width), F32) + 2 * _nbytes(state_shape, F32)
    return pl.pallas_call(
        functools.partial(_hgrn_kernel, layer=layer, chunk=chunk, sub=sub),
        grid=(b // bb, hg, t // tc),
        in_specs=[
            col(0), col(1), col(2), col(3),
            pl.BlockSpec((nl, width), lambda bi, hi, ti: (0, hi)),
            pl.BlockSpec((1, HEAD_DIM), lambda bi, hi, ti: (0, 0)),
            state_spec,
        ],
        out_specs=[
            pl.BlockSpec((bb, tc, width), lambda bi, hi, ti: (bi, ti, hi)),
            state_spec,
        ],
        out_shape=[
            jax.ShapeDtypeStruct((b, t, d), F32),
            jax.ShapeDtypeStruct(s0.shape, F32),
        ],
        scratch_shapes=[pltpu.VMEM(state_shape, F32)],
        compiler_params=_params(("parallel", "parallel", "arbitrary"), blocks, _nbytes(state_shape, F32)),
        name="hgrn",
    )(proj, proj, proj, proj, a_lb, g_norm.reshape(1, HEAD_DIM), s0)


def _sb_prompt_kernel(bias_ref, q_ref, k_ref, v_ref, o_ref, *, bq, bk, scale):
    group = pl.program_id(1)
    qi = pl.program_id(2)
    hp = q_ref.shape[1] // HEAD_DIM
    nsub = bq // bk
    heads = [slice(hh * HEAD_DIM, (hh + 1) * HEAD_DIM) for hh in range(hp)]
    bias = [bias_ref[group * hp + hh] * LOG2E for hh in range(hp)]
    q = [(q_ref[:, cols] * (scale * LOG2E)).astype(BF16) for cols in heads]
    cw = bk
    row = lax.broadcasted_iota(jnp.int32, (cw, cw), 0)
    colm = lax.broadcasted_iota(jnp.int32, (cw, cw), 1)
    suffix = (row >= colm).astype(BF16)
    q_pos = lax.broadcasted_iota(jnp.int32, (bq, bk), 0)
    k_pos = lax.broadcasted_iota(jnp.int32, (bq, bk), 1)

    def visit(kb, carry, mask):
        rows = pl.ds(pl.multiple_of(kb * bk, bk), bk)
        out = []
        for hh, cols in enumerate(heads):
            acc, r = carry[hh]
            k = k_ref[rows, cols].astype(BF16)
            v = v_ref[rows, cols].astype(BF16)
            z = _dot(q[hh], k, NT_DIMS) + bias[hh]
            log_rest = -_softplus2(z)
            if mask is not None:
                log_rest = jnp.where(mask, log_rest, 0.0)
            lr = log_rest.astype(BF16)
            parts, newer = [], r
            for cb in reversed(range(bk // cw)):
                kcols = slice(cb * cw, (cb + 1) * cw)
                csum = _dot(lr[:, kcols], suffix) + newer
                parts.insert(0, csum)
                newer = csum[:, 0:1]
            a = jnp.exp2(z + (parts[0] if len(parts) == 1 else jnp.concatenate(parts, axis=1)))
            if mask is not None:
                a = jnp.where(mask, a, 0.0)
            out.append((acc + _dot(a.astype(BF16), v), newer))
        return tuple(out)

    carry = ((jnp.zeros((bq, HEAD_DIM), F32), jnp.zeros((bq, 1), F32)),) * hp
    for sub in reversed(range(nsub)):
        carry = visit(qi * nsub + sub, carry, k_pos + sub * bk < q_pos)
    carry = lax.fori_loop(0, qi * nsub, lambda step, c: visit(qi * nsub - 1 - step, c, None), carry)
    for hh, cols in enumerate(heads):
        o_ref[:, cols] = carry[hh][0].astype(o_ref.dtype)


def _sb_prompt(q, k, v, bias):
    b, t, d = q.shape
    bq, bk = min(SB_BLOCK_Q, t), min(SB_BLOCK_K, t)
    width = SB_HEADS_PER_STEP * HEAD_DIM
    assert t % bq == 0 and bq % bk == 0 and d % width == 0
    kv_spec = pl.BlockSpec((None, t, width), lambda bi, hi, qi: (bi, 0, hi))
    q_spec = pl.BlockSpec((None, bq, width), lambda bi, hi, qi: (bi, qi, hi))
    blocks = 2 * _nbytes((t, width), F32) + _nbytes((bq, width), F32) + _nbytes((bq, width), BF16)
    return pl.pallas_call(
        functools.partial(_sb_prompt_kernel, bq=bq, bk=bk, scale=HEAD_DIM ** -0.5),
        grid=(b, d // width, t // bq),
        in_specs=[pl.BlockSpec(memory_space=pltpu.SMEM), q_spec, kv_spec, kv_spec],
        out_specs=q_spec,
        out_shape=jax.ShapeDtypeStruct((b, t, d), BF16),
        compiler_params=_params(("parallel", "parallel", "arbitrary"), blocks),
        name="sb_prompt",
    )(bias, q, k, v)


def _slab_head(slab_ref, hh):
    keys, nh, dim = slab_ref.shape
    return slab_ref.reshape(keys * nh, dim)[pl.ds(hh, keys, stride=nh), :]


def _sb_sample_kernel(pt_ref, q_ref, kn_ref, vn_ref, bias_ref, *rest, pages, scale):
    nq, d = q_ref.shape
    h = d // HEAD_DIM
    groups = h // SUBLANES
    n_slabs = pages * groups
    k_refs, v_refs = rest[:n_slabs], rest[n_slabs:2 * n_slabs]
    o_ref, qbd_ref, kc_ref, vc_ref, acc_ref, r_ref = rest[2 * n_slabs:]
    step = pl.program_id(1)
    page = k_refs[0].shape[0]
    lanes = h * nq
    heads = [slice(hh * HEAD_DIM, (hh + 1) * HEAD_DIM) for hh in range(h)]

    def visit(n, mask):
        z = _dot(kc_ref[0:n, :], qbd_ref[...], NT_DIMS) + bias_ref[...]
        log_rest = -_softplus(z)
        if mask is not None:
            log_rest = jnp.where(mask, log_rest, 0.0)
        cw = min(SB_CUMSUM_BLOCK, n)
        ri = lax.broadcasted_iota(jnp.int32, (cw, cw), 0)
        ci = lax.broadcasted_iota(jnp.int32, (cw, cw), 1)
        suffix = (ci >= ri).astype(BF16)
        hi, lo = _bf16_split(log_rest)
        parts, newer = [], r_ref[...]
        for cb in reversed(range(n // cw)):
            krows = slice(cb * cw, (cb + 1) * cw)
            csum = _dot(suffix, hi[krows]) + _dot(suffix, lo[krows]) + newer
            parts.insert(0, csum)
            newer = csum[0:1, :]
        a = jnp.exp(z + (parts[0] if len(parts) == 1 else jnp.concatenate(parts, axis=0)))
        if mask is not None:
            a = jnp.where(mask, a, 0.0)
        acc_ref[...] += _dot(jnp.transpose(a).astype(BF16), vc_ref[0:n, :])
        r_ref[...] = newer

    @pl.when(step == 0)
    def _():
        qs = q_ref[...] * scale
        qt = jnp.broadcast_to(qs[None], (h, nq, d)).reshape(lanes, d)
        rr = lax.broadcasted_iota(jnp.int32, (lanes, d), 0)
        cc = lax.broadcasted_iota(jnp.int32, (lanes, d), 1)
        qbd_ref[...] = jnp.where(rr // nq == cc // HEAD_DIM, qt, 0.0).astype(BF16)
        acc_ref[...] = jnp.zeros_like(acc_ref)
        r_ref[...] = jnp.zeros_like(r_ref)
        pad = jnp.zeros((page - nq, d), BF16)
        kc_ref[0:page, :] = jnp.concatenate([kn_ref[...].astype(BF16), pad], axis=0)
        vc_ref[0:page, :] = jnp.concatenate([vn_ref[...].astype(BF16), pad], axis=0)
        key = lax.broadcasted_iota(jnp.int32, (page, lanes), 0)
        query = lax.broadcasted_iota(jnp.int32, (page, lanes), 1) % nq
        visit(page, key < query)

    @pl.when(step > 0)
    def _():
        for p in range(pages):
            rows = slice((pages - 1 - p) * page, (pages - p) * page)
            for g in range(groups):
                for hh in range(SUBLANES):
                    cols = heads[g * SUBLANES + hh]
                    kc_ref[rows, cols] = _slab_head(k_refs[p * groups + g], hh).astype(BF16)
                    vc_ref[rows, cols] = _slab_head(v_refs[p * groups + g], hh).astype(BF16)
        visit(pages * page, None)

    @pl.when(step == pl.num_programs(1) - 1)
    def _():
        for hh, cols in enumerate(heads):
            o_ref[:, cols] = acc_ref[hh * nq:(hh + 1) * nq, cols]


def _sb_sample(q, k_new, v_new, bias, cache_k, cache_v, page_table):
    b, nq, d = q.shape
    n_pool, page, h, dh = cache_k.shape
    n_pages = page_table.shape[1]
    pages = min(SB_PAGES_PER_STEP, n_pages)
    assert n_pages % pages == 0 and h * nq == HEAD_DIM and nq <= page and dh == HEAD_DIM and h * dh == d and h % SUBLANES == 0
    lanes = h * nq

    def slab_spec(p, g):
        def index(bi, si, pt):
            group = jnp.maximum(si - 1, 0)
            return (pt[bi * n_pages + n_pages - 1 - (group * pages + p)], 0, g, 0)
        return pl.BlockSpec((None, page, SUBLANES, dh), index)

    slab_specs = [slab_spec(p, g) for p in range(pages) for g in range(h // SUBLANES)]

    tok_spec = pl.BlockSpec((None, nq, d), lambda bi, si, pt: (bi, 0, 0))
    scratch = [
        pltpu.VMEM((lanes, d), BF16),
        pltpu.VMEM((pages * page, d), BF16),
        pltpu.VMEM((pages * page, d), BF16),
        pltpu.VMEM((lanes, d), F32),
        pltpu.VMEM((1, lanes), F32),
    ]
    blocks = 2 * pages * _nbytes((page, d), F32) + 4 * _nbytes((nq, d), F32)
    scratch_bytes = (_nbytes((lanes, d), BF16) + 2 * _nbytes((pages * page, d), BF16)
                     + _nbytes((lanes, d), F32))
    grid_spec = pltpu.PrefetchScalarGridSpec(
        num_scalar_prefetch=1,
        grid=(b, 1 + n_pages // pages),
        in_specs=[tok_spec, tok_spec, tok_spec,
                  pl.BlockSpec((1, lanes), lambda bi, si, pt: (0, 0))]
                 + slab_specs * 2,
        out_specs=tok_spec,
        scratch_shapes=scratch,
    )
    return pl.pallas_call(
        functools.partial(_sb_sample_kernel, pages=pages, scale=HEAD_DIM ** -0.5),
        grid_spec=grid_spec,
        out_shape=jax.ShapeDtypeStruct((b, nq, d), F32),
        compiler_params=_params(("parallel", "arbitrary"), blocks, scratch_bytes),
        name="sb_sample",
    )(page_table.reshape(-1), q, k_new, v_new, jnp.repeat(bias, nq).reshape(1, lanes),
      *([cache_k] * len(slab_specs)), *([cache_v] * len(slab_specs)))


def _run_trunk(x3, p, hgrn_state, conv_state, attend, w, *, hgrn_bb, hgrn_hp, hgrn_tc, fuse_conv):
    b, t, d = x3.shape
    m = b * t
    depth = w["norm_mix"].shape[0]
    n_a = w["a_w_in"].shape[0]
    f = w["ffn_w_out"].shape[1]
    x = x3.reshape(m, d)
    new_s, new_c = [], []
    k_sh = v_sh = None
    for layer in range(depth):
        if layer < n_a:
            proj = _norm_matmul(x, w["norm_mix"][layer], w["a_w_in"][layer], n_out=4 * d)
            og, s = _hgrn(proj.reshape(b, t, 4 * d), w["a_lb"], w["a_g_norm"][layer], hgrn_state[layer],
                          layer=layer, bb=hgrn_bb, hp=hgrn_hp, tc=hgrn_tc)
            new_s.append(s)
            x = _matmul_res(og.reshape(m, d), w["a_w_out"][layer], x, tn=1024)
        else:
            bi = layer - n_a
            q = _norm_matmul(x, w["norm_mix"][layer], w["b_w_q"][bi], n_out=d, head_g=w["b_q_norm"][bi])
            o = attend(q.reshape(b, t, d), k_sh, v_sh, w["b_sb_bias"][bi])
            x = _matmul_res(o.reshape(m, d), w["b_w_out"][bi], x, tn=1024)
        if fuse_conv:
            act, cbuf = _ffn_in(x, w["norm_ffn"][layer], w["ffn_w_in"][layer], conv_state[layer],
                                w["ffn_conv_w"][layer], w["ffn_conv_b"][layer], seq_len=t)
        else:
            u = _norm_matmul(x, w["norm_ffn"][layer], w["ffn_w_in"][layer], n_out=2 * f)
            act, cbuf = _conv_gate(u.reshape(b, t, 2 * f), conv_state[layer], w["ffn_conv_w"][layer],
                                   w["ffn_conv_b"][layer], bb=b)
        new_c.append(cbuf)
        x = _matmul_res(act.reshape(m, f), w["ffn_w_out"][layer], x)
        x = _ple(x, p[layer].reshape(m, -1), w["ple_norm"][layer], w["ple_w"][layer], w["ple_w_gate"][layer])
        if layer == n_a - 1:
            k_sh, v_sh = _kv_proj(x, w["kv_norm"], w["kv_w"], w["k_norm"])
            k_sh, v_sh = k_sh.reshape(b, t, d), v_sh.reshape(b, t, d)
    return x.reshape(b, t, d), k_sh, v_sh, jnp.stack(new_s), jnp.stack(new_c)


def kernel(x_prompt, x_sample, p_prompt, p_sample, cache_k, cache_v, page_table, state_hgrn, state_conv, norm_mix, norm_ffn, a_w_in, a_lb, a_g_norm, a_w_out, kv_norm, kv_w, k_norm, b_w_q, b_q_norm, b_sb_bias, b_w_out, ffn_w_in, ffn_conv_w, ffn_conv_b, ffn_w_out, ple_w, ple_norm, ple_w_gate):
    w = dict(norm_mix=norm_mix, norm_ffn=norm_ffn, a_lb=a_lb, a_g_norm=a_g_norm, kv_norm=kv_norm,
             k_norm=k_norm, b_q_norm=b_q_norm, b_sb_bias=b_sb_bias, ffn_conv_w=ffn_conv_w,
             ffn_conv_b=ffn_conv_b, ple_norm=ple_norm)
    for name, val in dict(a_w_in=a_w_in, a_w_out=a_w_out, kv_w=kv_w, b_w_q=b_w_q, b_w_out=b_w_out,
                          ffn_w_in=ffn_w_in, ffn_w_out=ffn_w_out, ple_w=ple_w, ple_w_gate=ple_w_gate).items():
        w[name] = val.astype(BF16)

    b, t, d = x_prompt.shape
    heads = d // HEAD_DIM
    n_a = a_w_in.shape[0]
    depth = norm_mix.shape[0]
    f = ffn_w_out.shape[1]
    zero_h = jnp.zeros((n_a, b, heads, HEAD_DIM, HEAD_DIM), state_hgrn.dtype)
    zero_c = jnp.zeros((depth, b, state_conv.shape[2], f), state_conv.dtype)

    def attend_prompt(q, k, v, bias):
        return _sb_prompt(q, k, v, bias)

    def attend_sample(q, k, v, bias):
        return _sb_sample(q, k, v, bias, cache_k, cache_v, page_table)

    y_p, k_p, v_p, h_p, c_p = _run_trunk(x_prompt, p_prompt, zero_h, zero_c, attend_prompt, w,
                                         hgrn_bb=1, hgrn_hp=8, hgrn_tc=512, fuse_conv=True)
    y_s, k_s, v_s, h_s, c_s = _run_trunk(x_sample, p_sample, state_hgrn, state_conv, attend_sample, w,
                                         hgrn_bb=8, hgrn_hp=1, hgrn_tc=x_sample.shape[1], fuse_conv=False)
    hs = (heads, HEAD_DIM)
    return (y_p, y_s,
            k_p.reshape(k_p.shape[:2] + hs), v_p.reshape(v_p.shape[:2] + hs),
            k_s.reshape(k_s.shape[:2] + hs), v_s.reshape(v_s.shape[:2] + hs),
            h_p, h_s, c_p, c_s)
```

```python
import functools

import jax
import jax.numpy as jnp
from jax import lax
from jax.experimental import pallas as pl
from jax.experimental.pallas import tpu as pltpu

F32 = jnp.float32
BF16 = jnp.bfloat16
EPS = 1e-6
LOG2E = 1.4426950408889634
HEAD_DIM = 128
SUBLANES = 8
HGRN_CHUNK = 64
HGRN_SUB = 16
SB_BLOCK_Q = 512
SB_BLOCK_K = 512
SB_CUMSUM_BLOCK = 256
SB_PAGES_PER_STEP = 8
SB_HEADS_PER_STEP = 2
V7X_VMEM_CAP = 56 * 1024 * 1024
VMEM_TEMP_MARGIN = 16 * 1024 * 1024

NT_DIMS = (((1,), (1,)), ((), ()))
TN_DIMS = (((0,), (0,)), ((), ()))


def _params(semantics, block_bytes, scratch_bytes=0):
    limit = min(2 * block_bytes + scratch_bytes + VMEM_TEMP_MARGIN, V7X_VMEM_CAP)
    return pltpu.CompilerParams(dimension_semantics=semantics, vmem_limit_bytes=limit)


def _nbytes(shape, dtype):
    n = jnp.dtype(dtype).itemsize
    for s in shape:
        n *= s
    return n


def _rms(x, g):
    return x * lax.rsqrt(jnp.mean(x * x, axis=-1, keepdims=True) + EPS) * g


def _silu(x):
    return x * jax.nn.sigmoid(x)


def _softplus(z):
    return jnp.maximum(z, 0.0) + jnp.log(1.0 + jnp.exp(-jnp.abs(z)))


def _softplus2(z2):
    return jnp.maximum(z2, 0.0) + jnp.log2(1.0 + jnp.exp2(-jnp.abs(z2)))


def _bf16_split(x):
    hi = x.astype(BF16)
    lo = (x - hi.astype(F32)).astype(BF16)
    return hi, lo


def _dot(a, b, dims=None):
    if dims is None:
        return jnp.dot(a, b, preferred_element_type=F32)
    return lax.dot_general(a, b, dims, preferred_element_type=F32)


def _norm_matmul_kernel(x_ref, g_ref, w_ref, *rest, head_norm):
    if head_norm:
        hg_ref, o_ref, xn_ref = rest
    else:
        o_ref, xn_ref = rest

    @pl.when(pl.program_id(1) == 0)
    def _():
        xn_ref[...] = _rms(x_ref[...], g_ref[...]).astype(BF16)

    acc = _dot(xn_ref[...], w_ref[...])
    if head_norm:
        for c in range(acc.shape[1] // HEAD_DIM):
            cols = slice(c * HEAD_DIM, (c + 1) * HEAD_DIM)
            o_ref[:, cols] = _rms(acc[:, cols], hg_ref[...])
    else:
        o_ref[...] = acc


def _norm_matmul(x, g, w, *, n_out, head_g=None, tm=1024, tn=1024):
    m, k = x.shape
    tm, tn = min(tm, m), min(tn, n_out)
    assert m % tm == 0 and n_out % tn == 0 and w.shape[1] == n_out
    in_specs = [
        pl.BlockSpec((tm, k), lambda i, j: (i, 0)),
        pl.BlockSpec((1, k), lambda i, j: (0, 0)),
        pl.BlockSpec((k, tn), lambda i, j: (0, j)),
    ]
    args = [x, g.reshape(1, k), w]
    if head_g is not None:
        in_specs.append(pl.BlockSpec((1, HEAD_DIM), lambda i, j: (0, 0)))
        args.append(head_g.reshape(1, HEAD_DIM))
    blocks = _nbytes((tm, k), F32) + _nbytes((k, tn), BF16) + _nbytes((tm, tn), F32)
    return pl.pallas_call(
        functools.partial(_norm_matmul_kernel, head_norm=head_g is not None),
        grid=(m // tm, n_out // tn),
        in_specs=in_specs,
        out_specs=pl.BlockSpec((tm, tn), lambda i, j: (i, j)),
        out_shape=jax.ShapeDtypeStruct((m, n_out), F32),
        scratch_shapes=[pltpu.VMEM((tm, k), BF16)],
        compiler_params=_params(("parallel", "arbitrary"), blocks, _nbytes((tm, k), BF16)),
        name="norm_matmul_headnorm" if head_g is not None else "norm_matmul",
    )(*args)


def _kv_proj_kernel(x_ref, g_ref, wk_ref, wv_ref, hg_ref, k_ref, v_ref, xn_ref):
    @pl.when(pl.program_id(1) == 0)
    def _():
        xn_ref[...] = _rms(x_ref[...], g_ref[...]).astype(BF16)

    xn = xn_ref[...]
    acc = _dot(xn, wk_ref[...])
    for c in range(acc.shape[1] // HEAD_DIM):
        cols = slice(c * HEAD_DIM, (c + 1) * HEAD_DIM)
        k_ref[:, cols] = _rms(acc[:, cols], hg_ref[...])
    v_ref[...] = _dot(xn, wv_ref[...])


def _kv_proj(x, g, w, head_g, *, tm=1024, tn=512):
    m, d = x.shape
    n = w.shape[1] // 2
    tm, tn = min(tm, m), min(tn, n)
    assert m % tm == 0 and n % tn == 0 and tn % HEAD_DIM == 0
    nj = n // tn
    out_spec = pl.BlockSpec((tm, tn), lambda i, j: (i, j))
    blocks = _nbytes((tm, d), F32) + 2 * _nbytes((d, tn), BF16) + 2 * _nbytes((tm, tn), F32)
    return pl.pallas_call(
        _kv_proj_kernel,
        grid=(m // tm, nj),
        in_specs=[
            pl.BlockSpec((tm, d), lambda i, j: (i, 0)),
            pl.BlockSpec((1, d), lambda i, j: (0, 0)),
            pl.BlockSpec((d, tn), lambda i, j: (0, j)),
            pl.BlockSpec((d, tn), lambda i, j: (0, j + nj)),
            pl.BlockSpec((1, HEAD_DIM), lambda i, j: (0, 0)),
        ],
        out_specs=[out_spec, out_spec],
        out_shape=[jax.ShapeDtypeStruct((m, n), F32)] * 2,
        scratch_shapes=[pltpu.VMEM((tm, d), BF16)],
        compiler_params=_params(("parallel", "arbitrary"), blocks, _nbytes((tm, d), BF16)),
        name="kv_proj",
    )(x, g.reshape(1, d), w, w, head_g.reshape(1, HEAD_DIM))


def _ffn_in_kernel(x_ref, g_ref, wg_ref, wu_ref, buf_ref, cw_ref, cb_ref, act_ref, st_ref, xn_ref, tail_ref,
                   *, tiles_per_seq):
    i, j = pl.program_id(0), pl.program_id(1)
    tm = x_ref.shape[0]

    @pl.when(j == 0)
    def _():
        xn_ref[...] = _rms(x_ref[...], g_ref[...]).astype(BF16)

    @pl.when(i % tiles_per_seq == 0)
    def _():
        tail_ref[j, SUBLANES - 2:SUBLANES, :] = buf_ref[...]

    xn = xn_ref[...]
    ug = _dot(xn, wg_ref[...])
    t = lax.broadcasted_iota(jnp.int32, ug.shape, 0)
    h1 = tail_ref[j, SUBLANES - 1:SUBLANES, :]
    h2 = tail_ref[j, SUBLANES - 2:SUBLANES - 1, :]
    xm1 = jnp.where(t == 0, h1, pltpu.roll(ug, 1, axis=0))
    xm2 = jnp.where(t == 0, h2, jnp.where(t == 1, h1, pltpu.roll(ug, 2, axis=0)))
    conv = cb_ref[...] + xm2 * cw_ref[0:1, :] + xm1 * cw_ref[1:2, :] + ug * cw_ref[2:3, :]
    act_ref[...] = (_silu(conv) * _dot(xn, wu_ref[...])).astype(act_ref.dtype)
    tail_ref[j] = ug[tm - SUBLANES:tm, :]
    st_ref[...] = ug[tm - 2:tm, :]


def _ffn_in(x, g, w, buf, conv_w, conv_b, *, seq_len, tm=1024, tn=512):
    m, d = x.shape
    f = w.shape[1] // 2
    tm = min(tm, seq_len)
    assert seq_len % tm == 0 and m % seq_len == 0 and f % tn == 0 and buf.shape[1] == 2 and tm >= SUBLANES
    tiles_per_seq = seq_len // tm
    nj = f // tn
    blocks = (_nbytes((tm, d), F32) + 2 * _nbytes((d, tn), BF16) + _nbytes((tm, tn), BF16)
              + 2 * _nbytes((2, tn), F32))
    scratch_bytes = _nbytes((tm, d), BF16) + _nbytes((nj, SUBLANES, tn), F32)
    act, tails = pl.pallas_call(
        functools.partial(_ffn_in_kernel, tiles_per_seq=tiles_per_seq),
        grid=(m // tm, nj),
        in_specs=[
            pl.BlockSpec((tm, d), lambda i, j: (i, 0)),
            pl.BlockSpec((1, d), lambda i, j: (0, 0)),
            pl.BlockSpec((d, tn), lambda i, j: (0, j)),
            pl.BlockSpec((d, tn), lambda i, j: (0, j + nj)),
            pl.BlockSpec((None, 2, tn), lambda i, j: (i // tiles_per_seq, 0, j)),
            pl.BlockSpec((3, tn), lambda i, j: (0, j)),
            pl.BlockSpec((1, tn), lambda i, j: (0, j)),
        ],
        out_specs=[
            pl.BlockSpec((tm, tn), lambda i, j: (i, j)),
            pl.BlockSpec((None, 2, tn), lambda i, j: (i, 0, j)),
        ],
        out_shape=[
            jax.ShapeDtypeStruct((m, f), BF16),
            jax.ShapeDtypeStruct((m // tm, 2, f), F32),
        ],
        scratch_shapes=[pltpu.VMEM((tm, d), BF16), pltpu.VMEM((nj, SUBLANES, tn), F32)],
        compiler_params=_params(("arbitrary", "arbitrary"), blocks, scratch_bytes),
        name="ffn_in",
    )(x, g.reshape(1, d), w, w, buf, conv_w, conv_b.reshape(1, f))
    return act, tails[tiles_per_seq - 1::tiles_per_seq]


def _matmul_res_kernel(a_ref, w_ref, r_ref, o_ref):
    o_ref[...] = r_ref[...] + _dot(a_ref[...].astype(BF16), w_ref[...])


def _matmul_res(a, w, res, *, tm=1024, tn=512):
    m, k = a.shape
    n = w.shape[1]
    tm, tn = min(tm, m), min(tn, n)
    assert m % tm == 0 and n % tn == 0
    blocks = _nbytes((tm, k), a.dtype) + _nbytes((k, tn), BF16) + 2 * _nbytes((tm, tn), F32)
    return pl.pallas_call(
        _matmul_res_kernel,
        grid=(m // tm, n // tn),
        in_specs=[
            pl.BlockSpec((tm, k), lambda i, j: (i, 0)),
            pl.BlockSpec((k, tn), lambda i, j: (0, j)),
            pl.BlockSpec((tm, tn), lambda i, j: (i, j)),
        ],
        out_specs=pl.BlockSpec((tm, tn), lambda i, j: (i, j)),
        out_shape=jax.ShapeDtypeStruct((m, n), F32),
        compiler_params=_params(("parallel", "arbitrary"), blocks),
        name="matmul_res",
    )(a, w, res)


def _ple_kernel(x_ref, p_ref, g_ref, wpe_ref, wg_ref, o_ref, xn_ref):
    j = pl.program_id(1)

    @pl.when(j == 0)
    def _():
        xn_ref[...] = _rms(x_ref[...], g_ref[...]).astype(BF16)

    tn = o_ref.shape[1]
    gate = jax.nn.sigmoid(_dot(xn_ref[...], wg_ref[...]))
    pe = _dot(p_ref[...].astype(BF16), wpe_ref[...])
    o_ref[...] = x_ref[:, pl.ds(pl.multiple_of(j * tn, tn), tn)] + pe * gate


def _ple(x, p, layer, g, w_pe, w_gate, *, tm=512, tn=1024):
    m, d = x.shape
    pd = p.shape[2]
    tm, tn = min(tm, m), min(tn, d)
    assert m % tm == 0 and d % tn == 0
    blocks = (_nbytes((tm, d), F32) + _nbytes((tm, tn), F32) + _nbytes((tm, pd), F32)
              + _nbytes((pd, tn), BF16) + _nbytes((d, tn), BF16))
    return pl.pallas_call(
        _ple_kernel,
        grid=(m // tm, d // tn),
        in_specs=[
            pl.BlockSpec((tm, d), lambda i, j: (i, 0)),
            pl.BlockSpec((None, tm, pd), lambda i, j: (layer, i, 0)),
            pl.BlockSpec((1, d), lambda i, j: (0, 0)),
            pl.BlockSpec((pd, tn), lambda i, j: (0, j)),
            pl.BlockSpec((d, tn), lambda i, j: (0, j)),
        ],
        out_specs=pl.BlockSpec((tm, tn), lambda i, j: (i, j)),
        out_shape=jax.ShapeDtypeStruct((m, d), F32),
        scratch_shapes=[pltpu.VMEM((tm, d), BF16)],
        compiler_params=_params(("parallel", "arbitrary"), blocks, _nbytes((tm, d), BF16)),
        name="ple",
    )(x, p, g.reshape(1, d), w_pe, w_gate)


def _conv_gate_kernel(ug_ref, uu_ref, buf_ref, cw_ref, cb_ref, act_ref, st_ref):
    x = ug_ref[...]
    t_len = x.shape[1]
    t = lax.broadcasted_iota(jnp.int32, x.shape, 1)
    buf0 = buf_ref[:, 0:1, :]
    buf1 = buf_ref[:, 1:2, :]
    xm1 = jnp.where(t == 0, buf1, pltpu.roll(x, 1, axis=1))
    xm2 = jnp.where(t == 0, buf0, jnp.where(t == 1, buf1, pltpu.roll(x, 2, axis=1)))
    conv = cb_ref[...] + xm2 * cw_ref[0:1, :] + xm1 * cw_ref[1:2, :] + x * cw_ref[2:3, :]
    act_ref[...] = (_silu(conv) * uu_ref[...]).astype(act_ref.dtype)
    st_ref[...] = ug_ref[:, t_len - 2:t_len, :]


def _conv_gate(u, buf, conv_w, conv_b, *, bb, tk=512):
    b, t, f2 = u.shape
    f = f2 // 2
    assert b % bb == 0 and f % tk == 0 and t >= 2 and buf.shape[1] == 2
    nk = f // tk
    blocks = 2 * _nbytes((bb, t, tk), F32) + _nbytes((bb, t, tk), BF16) + 2 * _nbytes((bb, 2, tk), F32)
    return pl.pallas_call(
        _conv_gate_kernel,
        grid=(b // bb, nk),
        in_specs=[
            pl.BlockSpec((bb, t, tk), lambda i, j: (i, 0, j)),
            pl.BlockSpec((bb, t, tk), lambda i, j: (i, 0, j + nk)),
            pl.BlockSpec((bb, 2, tk), lambda i, j: (i, 0, j)),
            pl.BlockSpec((3, tk), lambda i, j: (0, j)),
            pl.BlockSpec((1, tk), lambda i, j: (0, j)),
        ],
        out_specs=[
            pl.BlockSpec((bb, t, tk), lambda i, j: (i, 0, j)),
            pl.BlockSpec((bb, 2, tk), lambda i, j: (i, 0, j)),
        ],
        out_shape=[
            jax.ShapeDtypeStruct((b, t, f), BF16),
            jax.ShapeDtypeStruct((b, 2, f), F32),
        ],
        compiler_params=_params(("parallel", "arbitrary"), blocks),
        name="conv_gate",
    )(u, u, buf, conv_w, conv_b.reshape(1, f))


def _cumsum_rows(x):
    n = x.shape[0]
    row = lax.broadcasted_iota(jnp.int32, x.shape, 0)
    s = 1
    while s < n:
        x = x + jnp.where(row >= s, pltpu.roll(x, s, axis=0), 0.0)
        s *= 2
    return x


def _hgrn_chunk(qz, fz, iv, state, lb, sub, state_t):
    c = qz.shape[0]
    f = lb + (1.0 - lb) * jax.nn.sigmoid(fz)
    q = _silu(qz)
    k = 1.0 - f
    b2 = _cumsum_rows(jnp.log2(f))
    b2_last = b2[c - 1:c, :]
    o = _dot((q * jnp.exp2(b2)).astype(BF16), state.astype(BF16), NT_DIMS if state_t else None)

    lane = lax.broadcasted_iota(jnp.int32, (sub, c), 1)
    rowi = lax.broadcasted_iota(jnp.int32, (sub, c), 0)
    att_rows = []
    for i in range(c // sub):
        r0 = i * sub
        bi, qi, ki = b2[r0:r0 + sub], q[r0:r0 + sub], k[r0:r0 + sub]
        att = jnp.zeros((sub, c), F32)
        for s in range(sub):
            pair = qi * ki[s:s + 1] * jnp.exp2(jnp.minimum(bi - bi[s:s + 1], 0.0))
            col = jnp.sum(pair, axis=1, keepdims=True)
            att = jnp.where((lane == r0 + s) & (rowi >= s), col, att)
        if i > 0:
            mid = bi[0:1]
            qt = qi * jnp.exp2(bi - mid)
            kt = k[:r0] * jnp.exp2(mid - b2[:r0])
            kt = jnp.concatenate([kt, jnp.zeros((c - r0, HEAD_DIM), F32)], axis=0)
            att = att + _dot(qt.astype(BF16), kt.astype(BF16), NT_DIMS)
        att_rows.append(att)
    att = att_rows[0] if len(att_rows) == 1 else jnp.concatenate(att_rows, axis=0)
    o = o + _dot(att.astype(BF16), iv.astype(BF16))

    k_st = (k * jnp.exp2(b2_last - b2)).astype(BF16)
    if state_t:
        new_state = jnp.exp2(b2_last) * state + _dot(iv.astype(BF16), k_st, TN_DIMS)
    else:
        decay = jnp.transpose(jnp.broadcast_to(jnp.exp2(b2_last), (HEAD_DIM, HEAD_DIM)))
        new_state = decay * state + _dot(k_st, iv.astype(BF16), TN_DIMS)
    return o, new_state


def _hgrn_kernel(q_ref, f_ref, i_ref, g_ref, alb_ref, gn_ref, s0_ref, og_ref, sout_ref, s_ref,
                 *, layer, chunk, sub):
    t_step = pl.program_id(2)
    bb, tc, width = q_ref.shape
    hp = width // HEAD_DIM
    state_t = pl.num_programs(2) * (tc // chunk) > 1

    def relayout(dst_ref, src_ref):
        for bi in range(bb):
            for hh in range(hp):
                dst_ref[bi, hh] = jnp.transpose(src_ref[bi, hh]) if state_t else src_ref[bi, hh]

    @pl.when(t_step == 0)
    def _():
        relayout(s_ref, s0_ref)

    a = alb_ref[...]
    e = jnp.exp(a - jnp.max(a, axis=0, keepdims=True))
    sm = e / jnp.sum(e, axis=0, keepdims=True)
    lb = jnp.sum(sm[:layer + 1], axis=0, keepdims=True)

    def chunk_body(ci, carry):
        rows = pl.ds(pl.multiple_of(ci * chunk, chunk), chunk)
        for bi in range(bb):
            for hh in range(hp):
                cols = slice(hh * HEAD_DIM, (hh + 1) * HEAD_DIM)
                o, state = _hgrn_chunk(q_ref[bi, rows, cols], f_ref[bi, rows, cols], i_ref[bi, rows, cols],
                                       s_ref[bi, hh], lb[:, cols], sub, state_t)
                og_ref[bi, rows, cols] = _rms(o, gn_ref[...]) * _silu(g_ref[bi, rows, cols])
                s_ref[bi, hh] = state
        return carry

    lax.fori_loop(0, tc // chunk, chunk_body, 0)

    @pl.when(t_step == pl.num_programs(2) - 1)
    def _():
        relayout(sout_ref, s_ref)


def _hgrn(proj, a_lb, g_norm, s0, *, layer, bb, hp, tc):
    b, t, d4 = proj.shape
    d = d4 // 4
    h = d // HEAD_DIM
    chunk = min(HGRN_CHUNK, t)
    sub = min(HGRN_SUB, chunk)
    assert b % bb == 0 and h % hp == 0 and t % tc == 0 and tc % chunk == 0 and chunk % sub == 0
    nl = a_lb.shape[0]
    width = hp * HEAD_DIM
    hg = h // hp

    def col(off):
        return pl.BlockSpec((bb, tc, width), lambda bi, hi, ti: (bi, ti, hi + off * hg))

    state_shape = (bb, hp, HEAD_DIM, HEAD_DIM)
    state_spec = pl.BlockSpec(state_shape, lambda bi, hi, ti: (bi, hi, 0, 0))
    blocks = 5 * _nbytes((bb, tc, width), F32) + 2 * _nbytes(state_shape, F32)
    return pl.pallas_call(
        functools.partial(_hgrn_kernel, layer=layer, chunk=chunk, sub=sub),
        grid=(b // bb, hg, t // tc),
        in_specs=[
            col(0), col(1), col(2), col(3),
            pl.BlockSpec((nl, width), lambda bi, hi, ti: (0, hi)),
            pl.BlockSpec((1, HEAD_DIM), lambda bi, hi, ti: (0, 0)),
            state_spec,
        ],
        out_specs=[
            pl.BlockSpec((bb, tc, width), lambda bi, hi, ti: (bi, ti, hi)),
            state_spec,
        ],
        out_shape=[
            jax.ShapeDtypeStruct((b, t, d), F32),
            jax.ShapeDtypeStruct(s0.shape, F32),
        ],
        scratch_shapes=[pltpu.VMEM(state_shape, F32)],
        compiler_params=_params(("parallel", "parallel", "arbitrary"), blocks, _nbytes(state_shape, F32)),
        name="hgrn",
    )(proj, proj, proj, proj, a_lb, g_norm.reshape(1, HEAD_DIM), s0)


def _sb_prompt_kernel(bias_ref, q_ref, k_ref, v_ref, o_ref, *, bq, bk, scale):
    group = pl.program_id(1)
    qi = pl.program_id(2)
    hp = q_ref.shape[1] // HEAD_DIM
    nsub = bq // bk
    heads = [slice(hh * HEAD_DIM, (hh + 1) * HEAD_DIM) for hh in range(hp)]
    bias = [bias_ref[group * hp + hh] * LOG2E for hh in range(hp)]
    q = [(q_ref[:, cols] * (scale * LOG2E)).astype(BF16) for cols in heads]
    cw = bk
    row = lax.broadcasted_iota(jnp.int32, (cw, cw), 0)
    colm = lax.broadcasted_iota(jnp.int32, (cw, cw), 1)
    suffix = (row >= colm).astype(BF16)
    q_pos = lax.broadcasted_iota(jnp.int32, (bq, bk), 0)
    k_pos = lax.broadcasted_iota(jnp.int32, (bq, bk), 1)

    def visit(kb, carry, mask):
        rows = pl.ds(pl.multiple_of(kb * bk, bk), bk)
        out = []
        for hh, cols in enumerate(heads):
            acc, r = carry[hh]
            k = k_ref[rows, cols].astype(BF16)
            v = v_ref[rows, cols].astype(BF16)
            z = _dot(q[hh], k, NT_DIMS) + bias[hh]
            log_rest = -_softplus2(z)
            if mask is not None:
                log_rest = jnp.where(mask, log_rest, 0.0)
            lr = log_rest.astype(BF16)
            parts, newer = [], r
            for cb in reversed(range(bk // cw)):
                kcols = slice(cb * cw, (cb + 1) * cw)
                csum = _dot(lr[:, kcols], suffix) + newer
                parts.insert(0, csum)
                newer = csum[:, 0:1]
            a = jnp.exp2(z + (parts[0] if len(parts) == 1 else jnp.concatenate(parts, axis=1)))
            if mask is not None:
                a = jnp.where(mask, a, 0.0)
            out.append((acc + _dot(a.astype(BF16), v), newer))
        return tuple(out)

    carry = ((jnp.zeros((bq, HEAD_DIM), F32), jnp.zeros((bq, 1), F32)),) * hp
    for sub in reversed(range(nsub)):
        carry = visit(qi * nsub + sub, carry, k_pos + sub * bk < q_pos)
    carry = lax.fori_loop(0, qi * nsub, lambda step, c: visit(qi * nsub - 1 - step, c, None), carry)
    for hh, cols in enumerate(heads):
        o_ref[:, cols] = carry[hh][0].astype(o_ref.dtype)


def _sb_prompt(q, k, v, bias):
    b, t, d = q.shape
    bq, bk = min(SB_BLOCK_Q, t), min(SB_BLOCK_K, t)
    width = SB_HEADS_PER_STEP * HEAD_DIM
    assert t % bq == 0 and bq % bk == 0 and d % width == 0
    kv_spec = pl.BlockSpec((None, t, width), lambda bi, hi, qi: (bi, 0, hi))
    q_spec = pl.BlockSpec((None, bq, width), lambda bi, hi, qi: (bi, qi, hi))
    blocks = 2 * _nbytes((t, width), F32) + _nbytes((bq, width), F32) + _nbytes((bq, width), BF16)
    return pl.pallas_call(
        functools.partial(_sb_prompt_kernel, bq=bq, bk=bk, scale=HEAD_DIM ** -0.5),
        grid=(b, d // width, t // bq),
        in_specs=[pl.BlockSpec(memory_space=pltpu.SMEM), q_spec, kv_spec, kv_spec],
        out_specs=q_spec,
        out_shape=jax.ShapeDtypeStruct((b, t, d), BF16),
        compiler_params=_params(("parallel", "parallel", "arbitrary"), blocks),
        name="sb_prompt",
    )(bias, q, k, v)


def _slab_head(slab_ref, hh):
    keys, nh, dim = slab_ref.shape
    return slab_ref.reshape(keys * nh, dim)[pl.ds(hh, keys, stride=nh), :]


def _sb_sample_kernel(pt_ref, q_ref, kn_ref, vn_ref, bias_ref, *rest, pages, scale):
    nq, d = q_ref.shape
    h = d // HEAD_DIM
    groups = h // SUBLANES
    n_slabs = pages * groups
    k_refs, v_refs = rest[:n_slabs], rest[n_slabs:2 * n_slabs]
    o_ref, qbd_ref, kc_ref, vc_ref, acc_ref, r_ref = rest[2 * n_slabs:]
    step = pl.program_id(1)
    page = k_refs[0].shape[0]
    lanes = h * nq
    heads = [slice(hh * HEAD_DIM, (hh + 1) * HEAD_DIM) for hh in range(h)]

    def visit(n, mask):
        z = _dot(kc_ref[0:n, :], qbd_ref[...], NT_DIMS) + bias_ref[...]
        log_rest = -_softplus(z)
        if mask is not None:
            log_rest = jnp.where(mask, log_rest, 0.0)
        cw = min(SB_CUMSUM_BLOCK, n)
        ri = lax.broadcasted_iota(jnp.int32, (cw, cw), 0)
        ci = lax.broadcasted_iota(jnp.int32, (cw, cw), 1)
        suffix = (ci >= ri).astype(BF16)
        hi, lo = _bf16_split(log_rest)
        parts, newer = [], r_ref[...]
        for cb in reversed(range(n // cw)):
            krows = slice(cb * cw, (cb + 1) * cw)
            csum = _dot(suffix, hi[krows]) + _dot(suffix, lo[krows]) + newer
            parts.insert(0, csum)
            newer = csum[0:1, :]
        a = jnp.exp(z + (parts[0] if len(parts) == 1 else jnp.concatenate(parts, axis=0)))
        if mask is not None:
            a = jnp.where(mask, a, 0.0)
        acc_ref[...] += _dot(jnp.transpose(a).astype(BF16), vc_ref[0:n, :])
        r_ref[...] = newer

    @pl.when(step == 0)
    def _():
        qs = q_ref[...] * scale
        qt = jnp.broadcast_to(qs[None], (h, nq, d)).reshape(lanes, d)
        rr = lax.broadcasted_iota(jnp.int32, (lanes, d), 0)
        cc = lax.broadcasted_iota(jnp.int32, (lanes, d), 1)
        qbd_ref[...] = jnp.where(rr // nq == cc // HEAD_DIM, qt, 0.0).astype(BF16)
        acc_ref[...] = jnp.zeros_like(acc_ref)
        r_ref[...] = jnp.zeros_like(r_ref)
        pad = jnp.zeros((page - nq, d), BF16)
        kc_ref[0:page, :] = jnp.concatenate([kn_ref[...].astype(BF16), pad], axis=0)
        vc_ref[0:page, :] = jnp.concatenate([vn_ref[...].astype(BF16), pad], axis=0)
        key = lax.broadcasted_iota(jnp.int32, (page, lanes), 0)
        query = lax.broadcasted_iota(jnp.int32, (page, lanes), 1) % nq
        visit(page, key < query)

    @pl.when(step > 0)
    def _():
        for p in range(pages):
            rows = slice((pages - 1 - p) * page, (pages - p) * page)
            for g in range(groups):
                for hh in range(SUBLANES):
                    cols = heads[g * SUBLANES + hh]
                    kc_ref[rows, cols] = _slab_head(k_refs[p * groups + g], hh).astype(BF16)
                    vc_ref[rows, cols] = _slab_head(v_refs[p * groups + g], hh).astype(BF16)
        visit(pages * page, None)

    @pl.when(step == pl.num_programs(1) - 1)
    def _():
        for hh, cols in enumerate(heads):
            o_ref[:, cols] = acc_ref[hh * nq:(hh + 1) * nq, cols]


def _sb_sample(q, k_new, v_new, bias, cache_k, cache_v, page_table):
    b, nq, d = q.shape
    n_pool, page, h, dh = cache_k.shape
    n_pages = page_table.shape[1]
    pages = min(SB_PAGES_PER_STEP, n_pages)
    assert n_pages % pages == 0 and h * nq == HEAD_DIM and nq <= page and dh == HEAD_DIM and h * dh == d and h % SUBLANES == 0
    lanes = h * nq

    def slab_spec(p, g):
        def index(bi, si, pt):
            group = jnp.maximum(si - 1, 0)
            return (pt[bi * n_pages + n_pages - 1 - (group * pages + p)], 0, g, 0)
        return pl.BlockSpec((None, page, SUBLANES, dh), index)

    slab_specs = [slab_spec(p, g) for p in range(pages) for g in range(h // SUBLANES)]

    tok_spec = pl.BlockSpec((None, nq, d), lambda bi, si, pt: (bi, 0, 0))
    scratch = [
        pltpu.VMEM((lanes, d), BF16),
        pltpu.VMEM((pages * page, d), BF16),
        pltpu.VMEM((pages * page, d), BF16),
        pltpu.VMEM((lanes, d), F32),
        pltpu.VMEM((1, lanes), F32),
    ]
    blocks = 2 * pages * _nbytes((page, d), F32) + 4 * _nbytes((nq, d), F32)
    scratch_bytes = (_nbytes((lanes, d), BF16) + 2 * _nbytes((pages * page, d), BF16)
                     + _nbytes((lanes, d), F32))
    grid_spec = pltpu.PrefetchScalarGridSpec(
        num_scalar_prefetch=1,
        grid=(b, 1 + n_pages // pages),
        in_specs=[tok_spec, tok_spec, tok_spec,
                  pl.BlockSpec((1, lanes), lambda bi, si, pt: (0, 0))]
                 + slab_specs * 2,
        out_specs=tok_spec,
        scratch_shapes=scratch,
    )
    return pl.pallas_call(
        functools.partial(_sb_sample_kernel, pages=pages, scale=HEAD_DIM ** -0.5),
        grid_spec=grid_spec,
        out_shape=jax.ShapeDtypeStruct((b, nq, d), F32),
        compiler_params=_params(("parallel", "arbitrary"), blocks, scratch_bytes),
        name="sb_sample",
    )(page_table.reshape(-1), q, k_new, v_new, jnp.repeat(bias, nq).reshape(1, lanes),
      *([cache_k] * len(slab_specs)), *([cache_v] * len(slab_specs)))


def _run_trunk(x3, p, hgrn_state, conv_state, attend, w, *, hgrn_bb, hgrn_hp, hgrn_tc, fuse_conv):
    b, t, d = x3.shape
    m = b * t
    depth = w["norm_mix"].shape[0]
    n_a = len(w["a_w_in"])
    f = w["ffn_w_out"][0].shape[0]
    x = x3.reshape(m, d)
    new_s, new_c = [], []
    k_sh = v_sh = None
    for layer in range(depth):
        if layer < n_a:
            proj = _norm_matmul(x, w["norm_mix"][layer], w["a_w_in"][layer], n_out=4 * d)
            og, s = _hgrn(proj.reshape(b, t, 4 * d), w["a_lb"], w["a_g_norm"][layer], hgrn_state[layer],
                          layer=layer, bb=hgrn_bb, hp=hgrn_hp, tc=hgrn_tc)
            new_s.append(s)
            x = _matmul_res(og.reshape(m, d), w["a_w_out"][layer], x, tn=1024)
        else:
            bi = layer - n_a
            q = _norm_matmul(x, w["norm_mix"][layer], w["b_w_q"][bi], n_out=d, head_g=w["b_q_norm"][bi])
            o = attend(q.reshape(b, t, d), k_sh, v_sh, w["b_sb_bias"][bi])
            x = _matmul_res(o.reshape(m, d), w["b_w_out"][bi], x, tn=1024)
        if fuse_conv:
            act, cbuf = _ffn_in(x, w["norm_ffn"][layer], w["ffn_w_in"][layer], conv_state[layer],
                                w["ffn_conv_w"][layer], w["ffn_conv_b"][layer], seq_len=t)
        else:
            u = _norm_matmul(x, w["norm_ffn"][layer], w["ffn_w_in"][layer], n_out=2 * f)
            act, cbuf = _conv_gate(u.reshape(b, t, 2 * f), conv_state[layer], w["ffn_conv_w"][layer],
                                   w["ffn_conv_b"][layer], bb=b)
        new_c.append(cbuf)
        x = _matmul_res(act.reshape(m, f), w["ffn_w_out"][layer], x)
        x = _ple(x, p.reshape(depth, m, -1), layer, w["ple_norm"][layer], w["ple_w"][layer],
                 w["ple_w_gate"][layer])
        if layer == n_a - 1:
            k_sh, v_sh = _kv_proj(x, w["kv_norm"], w["kv_w"], w["k_norm"])
            k_sh, v_sh = k_sh.reshape(b, t, d), v_sh.reshape(b, t, d)
    return x.reshape(b, t, d), k_sh, v_sh, jnp.stack(new_s), jnp.stack(new_c)


def kernel(x_prompt, x_sample, p_prompt, p_sample, cache_k, cache_v, page_table, state_hgrn, state_conv, norm_mix, norm_ffn, a_w_in, a_lb, a_g_norm, a_w_out, kv_norm, kv_w, k_norm, b_w_q, b_q_norm, b_sb_bias, b_w_out, ffn_w_in, ffn_conv_w, ffn_conv_b, ffn_w_out, ple_w, ple_norm, ple_w_gate):
    w = dict(norm_mix=norm_mix, norm_ffn=norm_ffn, a_lb=a_lb, a_g_norm=a_g_norm, kv_norm=kv_norm,
             k_norm=k_norm, b_q_norm=b_q_norm, b_sb_bias=b_sb_bias, ffn_conv_w=ffn_conv_w,
             ffn_conv_b=ffn_conv_b, ple_norm=ple_norm)
    for name, val in dict(a_w_in=a_w_in, a_w_out=a_w_out, b_w_q=b_w_q, b_w_out=b_w_out, ffn_w_in=ffn_w_in,
                          ffn_w_out=ffn_w_out, ple_w=ple_w, ple_w_gate=ple_w_gate).items():
        w[name] = [val[layer].astype(BF16) for layer in range(val.shape[0])]
    w["kv_w"] = kv_w.astype(BF16)

    b, t, d = x_prompt.shape
    heads = d // HEAD_DIM
    n_a = a_w_in.shape[0]
    depth = norm_mix.shape[0]
    f = ffn_w_out.shape[1]
    zero_h = jnp.zeros((n_a, b, heads, HEAD_DIM, HEAD_DIM), state_hgrn.dtype)
    zero_c = jnp.zeros((depth, b, state_conv.shape[2], f), state_conv.dtype)

    def attend_prompt(q, k, v, bias):
        return _sb_prompt(q, k, v, bias)

    def attend_sample(q, k, v, bias):
        return _sb_sample(q, k, v, bias, cache_k, cache_v, page_table)

    y_p, k_p, v_p, h_p, c_p = _run_trunk(x_prompt, p_prompt, zero_h, zero_c, attend_prompt, w,
                                         hgrn_bb=1, hgrn_hp=8, hgrn_tc=512, fuse_conv=True)
    y_s, k_s, v_s, h_s, c_s = _run_trunk(x_sample, p_sample, state_hgrn, state_conv, attend_sample, w,
                                         hgrn_bb=8, hgrn_hp=1, hgrn_tc=x_sample.shape[1], fuse_conv=False)
    hs = (heads, HEAD_DIM)
    return (y_p, y_s,
            k_p.reshape(k_p.shape[:2] + hs), v_p.reshape(v_p.shape[:2] + hs),
            k_s.reshape(k_s.shape[:2] + hs), v_s.reshape(v_s.shape[:2] + hs),
            h_p, h_s, c_p, c_s)
```

```python
import functools

import jax
import jax.numpy as jnp
from jax import lax
from jax.experimental import pallas as pl
from jax.experimental.pallas import tpu as pltpu

F32 = jnp.float32
BF16 = jnp.bfloat16
EPS = 1e-6
LOG2E = 1.4426950408889634
HEAD_DIM = 128
SUBLANES = 8
HGRN_CHUNK = 64
HGRN_SUB = 16
SB_BLOCK_Q = 512
SB_BLOCK_K = 512
SB_CUMSUM_BLOCK = 256
SB_PAGES_PER_STEP = 8
SB_HEADS_PER_STEP = 2
V7X_VMEM_CAP = 56 * 1024 * 1024
VMEM_TEMP_MARGIN = 16 * 1024 * 1024

NT_DIMS = (((1,), (1,)), ((), ()))
TN_DIMS = (((0,), (0,)), ((), ()))


def _params(semantics, block_bytes, scratch_bytes=0):
    limit = min(2 * block_bytes + scratch_bytes + VMEM_TEMP_MARGIN, V7X_VMEM_CAP)
    return pltpu.CompilerParams(dimension_semantics=semantics, vmem_limit_bytes=limit)


def _nbytes(shape, dtype):
    n = jnp.dtype(dtype).itemsize
    for s in shape:
        n *= s
    return n


def _rms(x, g):
    return x * lax.rsqrt(jnp.mean(x * x, axis=-1, keepdims=True) + EPS) * g


def _silu(x):
    return x * jax.nn.sigmoid(x)


def _softplus(z):
    return jnp.maximum(z, 0.0) + jnp.log(1.0 + jnp.exp(-jnp.abs(z)))


def _softplus2(z2):
    return jnp.maximum(z2, 0.0) + jnp.log2(1.0 + jnp.exp2(-jnp.abs(z2)))


def _bf16_split(x):
    hi = x.astype(BF16)
    lo = (x - hi.astype(F32)).astype(BF16)
    return hi, lo


def _w_spec(w, layer, tn, off=0):
    if layer is None:
        return pl.BlockSpec((w.shape[0], tn), lambda i, j: (0, j + off))
    return pl.BlockSpec((None, w.shape[1], tn), lambda i, j: (layer, 0, j + off))


def _dot(a, b, dims=None):
    if dims is None:
        return jnp.dot(a, b, preferred_element_type=F32)
    return lax.dot_general(a, b, dims, preferred_element_type=F32)


def _norm_matmul_kernel(x_ref, g_ref, w_ref, *rest, head_norm):
    if head_norm:
        hg_ref, o_ref, xn_ref = rest
    else:
        o_ref, xn_ref = rest

    @pl.when(pl.program_id(1) == 0)
    def _():
        xn_ref[...] = _rms(x_ref[...], g_ref[...]).astype(BF16)

    acc = _dot(xn_ref[...], w_ref[...])
    if head_norm:
        for c in range(acc.shape[1] // HEAD_DIM):
            cols = slice(c * HEAD_DIM, (c + 1) * HEAD_DIM)
            o_ref[:, cols] = _rms(acc[:, cols], hg_ref[...])
    else:
        o_ref[...] = acc


def _norm_matmul(x, g, w, *, n_out, layer=None, head_g=None, tm=1024, tn=1024):
    m, k = x.shape
    tm, tn = min(tm, m), min(tn, n_out)
    assert m % tm == 0 and n_out % tn == 0 and w.shape[-1] == n_out and w.shape[-2] == k
    in_specs = [
        pl.BlockSpec((tm, k), lambda i, j: (i, 0)),
        pl.BlockSpec((1, k), lambda i, j: (0, 0)),
        _w_spec(w, layer, tn),
    ]
    args = [x, g.reshape(1, k), w]
    if head_g is not None:
        in_specs.append(pl.BlockSpec((1, HEAD_DIM), lambda i, j: (0, 0)))
        args.append(head_g.reshape(1, HEAD_DIM))
    blocks = _nbytes((tm, k), F32) + _nbytes((k, tn), BF16) + _nbytes((tm, tn), F32)
    return pl.pallas_call(
        functools.partial(_norm_matmul_kernel, head_norm=head_g is not None),
        grid=(m // tm, n_out // tn),
        in_specs=in_specs,
        out_specs=pl.BlockSpec((tm, tn), lambda i, j: (i, j)),
        out_shape=jax.ShapeDtypeStruct((m, n_out), F32),
        scratch_shapes=[pltpu.VMEM((tm, k), BF16)],
        compiler_params=_params(("parallel", "arbitrary"), blocks, _nbytes((tm, k), BF16)),
        name="norm_matmul_headnorm" if head_g is not None else "norm_matmul",
    )(*args)


def _kv_proj_kernel(x_ref, g_ref, wk_ref, wv_ref, hg_ref, k_ref, v_ref, xn_ref):
    @pl.when(pl.program_id(1) == 0)
    def _():
        xn_ref[...] = _rms(x_ref[...], g_ref[...]).astype(BF16)

    xn = xn_ref[...]
    acc = _dot(xn, wk_ref[...])
    for c in range(acc.shape[1] // HEAD_DIM):
        cols = slice(c * HEAD_DIM, (c + 1) * HEAD_DIM)
        k_ref[:, cols] = _rms(acc[:, cols], hg_ref[...])
    v_ref[...] = _dot(xn, wv_ref[...])


def _kv_proj(x, g, w, head_g, *, tm=1024, tn=512):
    m, d = x.shape
    n = w.shape[1] // 2
    tm, tn = min(tm, m), min(tn, n)
    assert m % tm == 0 and n % tn == 0 and tn % HEAD_DIM == 0
    nj = n // tn
    out_spec = pl.BlockSpec((tm, tn), lambda i, j: (i, j))
    blocks = _nbytes((tm, d), F32) + 2 * _nbytes((d, tn), BF16) + 2 * _nbytes((tm, tn), F32)
    return pl.pallas_call(
        _kv_proj_kernel,
        grid=(m // tm, nj),
        in_specs=[
            pl.BlockSpec((tm, d), lambda i, j: (i, 0)),
            pl.BlockSpec((1, d), lambda i, j: (0, 0)),
            pl.BlockSpec((d, tn), lambda i, j: (0, j)),
            pl.BlockSpec((d, tn), lambda i, j: (0, j + nj)),
            pl.BlockSpec((1, HEAD_DIM), lambda i, j: (0, 0)),
        ],
        out_specs=[out_spec, out_spec],
        out_shape=[jax.ShapeDtypeStruct((m, n), F32)] * 2,
        scratch_shapes=[pltpu.VMEM((tm, d), BF16)],
        compiler_params=_params(("parallel", "arbitrary"), blocks, _nbytes((tm, d), BF16)),
        name="kv_proj",
    )(x, g.reshape(1, d), w, w, head_g.reshape(1, HEAD_DIM))


def _ffn_in_kernel(x_ref, g_ref, wg_ref, wu_ref, buf_ref, cw_ref, cb_ref, act_ref, st_ref, xn_ref, tail_ref,
                   *, tiles_per_seq):
    i, j = pl.program_id(0), pl.program_id(1)
    tm = x_ref.shape[0]

    @pl.when(j == 0)
    def _():
        xn_ref[...] = _rms(x_ref[...], g_ref[...]).astype(BF16)

    @pl.when(i % tiles_per_seq == 0)
    def _():
        tail_ref[j, SUBLANES - 2:SUBLANES, :] = buf_ref[...]

    xn = xn_ref[...]
    ug = _dot(xn, wg_ref[...])
    t = lax.broadcasted_iota(jnp.int32, ug.shape, 0)
    h1 = tail_ref[j, SUBLANES - 1:SUBLANES, :]
    h2 = tail_ref[j, SUBLANES - 2:SUBLANES - 1, :]
    xm1 = jnp.where(t == 0, h1, pltpu.roll(ug, 1, axis=0))
    xm2 = jnp.where(t == 0, h2, jnp.where(t == 1, h1, pltpu.roll(ug, 2, axis=0)))
    conv = cb_ref[...] + xm2 * cw_ref[0:1, :] + xm1 * cw_ref[1:2, :] + ug * cw_ref[2:3, :]
    act_ref[...] = (_silu(conv) * _dot(xn, wu_ref[...])).astype(act_ref.dtype)
    tail_ref[j] = ug[tm - SUBLANES:tm, :]
    st_ref[...] = ug[tm - 2:tm, :]


def _ffn_in(x, g, w, buf, conv_w, conv_b, *, seq_len, layer=None, tm=1024, tn=512):
    m, d = x.shape
    f = w.shape[-1] // 2
    tm = min(tm, seq_len)
    assert seq_len % tm == 0 and m % seq_len == 0 and f % tn == 0 and buf.shape[1] == 2 and tm >= SUBLANES
    tiles_per_seq = seq_len // tm
    nj = f // tn
    blocks = (_nbytes((tm, d), F32) + 2 * _nbytes((d, tn), BF16) + _nbytes((tm, tn), BF16)
              + 2 * _nbytes((2, tn), F32))
    scratch_bytes = _nbytes((tm, d), BF16) + _nbytes((nj, SUBLANES, tn), F32)
    act, tails = pl.pallas_call(
        functools.partial(_ffn_in_kernel, tiles_per_seq=tiles_per_seq),
        grid=(m // tm, nj),
        in_specs=[
            pl.BlockSpec((tm, d), lambda i, j: (i, 0)),
            pl.BlockSpec((1, d), lambda i, j: (0, 0)),
            _w_spec(w, layer, tn),
            _w_spec(w, layer, tn, nj),
            pl.BlockSpec((None, 2, tn), lambda i, j: (i // tiles_per_seq, 0, j)),
            pl.BlockSpec((3, tn), lambda i, j: (0, j)),
            pl.BlockSpec((1, tn), lambda i, j: (0, j)),
        ],
        out_specs=[
            pl.BlockSpec((tm, tn), lambda i, j: (i, j)),
            pl.BlockSpec((None, 2, tn), lambda i, j: (i, 0, j)),
        ],
        out_shape=[
            jax.ShapeDtypeStruct((m, f), BF16),
            jax.ShapeDtypeStruct((m // tm, 2, f), F32),
        ],
        scratch_shapes=[pltpu.VMEM((tm, d), BF16), pltpu.VMEM((nj, SUBLANES, tn), F32)],
        compiler_params=_params(("arbitrary", "arbitrary"), blocks, scratch_bytes),
        name="ffn_in",
    )(x, g.reshape(1, d), w, w, buf, conv_w, conv_b.reshape(1, f))
    return act, tails[tiles_per_seq - 1::tiles_per_seq]


def _matmul_res_kernel(a_ref, w_ref, r_ref, o_ref):
    o_ref[...] = r_ref[...] + _dot(a_ref[...].astype(BF16), w_ref[...])


def _matmul_res(a, w, res, *, layer=None, tm=1024, tn=512):
    m, k = a.shape
    n = w.shape[-1]
    tm, tn = min(tm, m), min(tn, n)
    assert m % tm == 0 and n % tn == 0 and w.shape[-2] == k
    blocks = _nbytes((tm, k), a.dtype) + _nbytes((k, tn), BF16) + 2 * _nbytes((tm, tn), F32)
    return pl.pallas_call(
        _matmul_res_kernel,
        grid=(m // tm, n // tn),
        in_specs=[
            pl.BlockSpec((tm, k), lambda i, j: (i, 0)),
            _w_spec(w, layer, tn),
            pl.BlockSpec((tm, tn), lambda i, j: (i, j)),
        ],
        out_specs=pl.BlockSpec((tm, tn), lambda i, j: (i, j)),
        out_shape=jax.ShapeDtypeStruct((m, n), F32),
        compiler_params=_params(("parallel", "arbitrary"), blocks),
        name="matmul_res",
    )(a, w, res)


def _ple_kernel(x_ref, p_ref, g_ref, wpe_ref, wg_ref, o_ref, xn_ref):
    j = pl.program_id(1)

    @pl.when(j == 0)
    def _():
        xn_ref[...] = _rms(x_ref[...], g_ref[...]).astype(BF16)

    tn = o_ref.shape[1]
    gate = jax.nn.sigmoid(_dot(xn_ref[...], wg_ref[...]))
    pe = _dot(p_ref[...].astype(BF16), wpe_ref[...])
    o_ref[...] = x_ref[:, pl.ds(pl.multiple_of(j * tn, tn), tn)] + pe * gate


def _ple(x, p, layer, g, w_pe, w_gate, *, tm=512, tn=1024):
    m, d = x.shape
    pd = p.shape[2]
    tm, tn = min(tm, m), min(tn, d)
    assert m % tm == 0 and d % tn == 0
    blocks = (_nbytes((tm, d), F32) + _nbytes((tm, tn), F32) + _nbytes((tm, pd), F32)
              + _nbytes((pd, tn), BF16) + _nbytes((d, tn), BF16))
    return pl.pallas_call(
        _ple_kernel,
        grid=(m // tm, d // tn),
        in_specs=[
            pl.BlockSpec((tm, d), lambda i, j: (i, 0)),
            pl.BlockSpec((None, tm, pd), lambda i, j: (layer, i, 0)),
            pl.BlockSpec((1, d), lambda i, j: (0, 0)),
            pl.BlockSpec((pd, tn), lambda i, j: (0, j)),
            pl.BlockSpec((d, tn), lambda i, j: (0, j)),
        ],
        out_specs=pl.BlockSpec((tm, tn), lambda i, j: (i, j)),
        out_shape=jax.ShapeDtypeStruct((m, d), F32),
        scratch_shapes=[pltpu.VMEM((tm, d), BF16)],
        compiler_params=_params(("parallel", "arbitrary"), blocks, _nbytes((tm, d), BF16)),
        name="ple",
    )(x, p, g.reshape(1, d), w_pe, w_gate)


def _conv_gate_kernel(ug_ref, uu_ref, buf_ref, cw_ref, cb_ref, act_ref, st_ref):
    x = ug_ref[...]
    t_len = x.shape[1]
    t = lax.broadcasted_iota(jnp.int32, x.shape, 1)
    buf0 = buf_ref[:, 0:1, :]
    buf1 = buf_ref[:, 1:2, :]
    xm1 = jnp.where(t == 0, buf1, pltpu.roll(x, 1, axis=1))
    xm2 = jnp.where(t == 0, buf0, jnp.where(t == 1, buf1, pltpu.roll(x, 2, axis=1)))
    conv = cb_ref[...] + xm2 * cw_ref[0:1, :] + xm1 * cw_ref[1:2, :] + x * cw_ref[2:3, :]
    act_ref[...] = (_silu(conv) * uu_ref[...]).astype(act_ref.dtype)
    st_ref[...] = ug_ref[:, t_len - 2:t_len, :]


def _conv_gate(u, buf, conv_w, conv_b, *, bb, tk=512):
    b, t, f2 = u.shape
    f = f2 // 2
    assert b % bb == 0 and f % tk == 0 and t >= 2 and buf.shape[1] == 2
    nk = f // tk
    blocks = 2 * _nbytes((bb, t, tk), F32) + _nbytes((bb, t, tk), BF16) + 2 * _nbytes((bb, 2, tk), F32)
    return pl.pallas_call(
        _conv_gate_kernel,
        grid=(b // bb, nk),
        in_specs=[
            pl.BlockSpec((bb, t, tk), lambda i, j: (i, 0, j)),
            pl.BlockSpec((bb, t, tk), lambda i, j: (i, 0, j + nk)),
            pl.BlockSpec((bb, 2, tk), lambda i, j: (i, 0, j)),
            pl.BlockSpec((3, tk), lambda i, j: (0, j)),
            pl.BlockSpec((1, tk), lambda i, j: (0, j)),
        ],
        out_specs=[
            pl.BlockSpec((bb, t, tk), lambda i, j: (i, 0, j)),
            pl.BlockSpec((bb, 2, tk), lambda i, j: (i, 0, j)),
        ],
        out_shape=[
            jax.ShapeDtypeStruct((b, t, f), BF16),
            jax.ShapeDtypeStruct((b, 2, f), F32),
        ],
        compiler_params=_params(("parallel", "arbitrary"), blocks),
        name="conv_gate",
    )(u, u, buf, conv_w, conv_b.reshape(1, f))


def _cumsum_rows(x):
    n = x.shape[0]
    row = lax.broadcasted_iota(jnp.int32, x.shape, 0)
    s = 1
    while s < n:
        x = x + jnp.where(row >= s, pltpu.roll(x, s, axis=0), 0.0)
        s *= 2
    return x


def _hgrn_chunk(qz, fz, iv, state, lb, sub, state_t):
    c = qz.shape[0]
    f = lb + (1.0 - lb) * jax.nn.sigmoid(fz)
    q = _silu(qz)
    k = 1.0 - f
    b2 = _cumsum_rows(jnp.log2(f))
    b2_last = b2[c - 1:c, :]
    o = _dot((q * jnp.exp2(b2)).astype(BF16), state.astype(BF16), NT_DIMS if state_t else None)

    lane = lax.broadcasted_iota(jnp.int32, (sub, c), 1)
    rowi = lax.broadcasted_iota(jnp.int32, (sub, c), 0)
    att_rows = []
    for i in range(c // sub):
        r0 = i * sub
        bi, qi, ki = b2[r0:r0 + sub], q[r0:r0 + sub], k[r0:r0 + sub]
        att = jnp.zeros((sub, c), F32)
        for s in range(sub):
            pair = qi * ki[s:s + 1] * jnp.exp2(jnp.minimum(bi - bi[s:s + 1], 0.0))
            col = jnp.sum(pair, axis=1, keepdims=True)
            att = jnp.where((lane == r0 + s) & (rowi >= s), col, att)
        if i > 0:
            mid = bi[0:1]
            qt = qi * jnp.exp2(bi - mid)
            kt = k[:r0] * jnp.exp2(mid - b2[:r0])
            kt = jnp.concatenate([kt, jnp.zeros((c - r0, HEAD_DIM), F32)], axis=0)
            att = att + _dot(qt.astype(BF16), kt.astype(BF16), NT_DIMS)
        att_rows.append(att)
    att = att_rows[0] if len(att_rows) == 1 else jnp.concatenate(att_rows, axis=0)
    o = o + _dot(att.astype(BF16), iv.astype(BF16))

    k_st = (k * jnp.exp2(b2_last - b2)).astype(BF16)
    if state_t:
        new_state = jnp.exp2(b2_last) * state + _dot(iv.astype(BF16), k_st, TN_DIMS)
    else:
        decay = jnp.transpose(jnp.broadcast_to(jnp.exp2(b2_last), (HEAD_DIM, HEAD_DIM)))
        new_state = decay * state + _dot(k_st, iv.astype(BF16), TN_DIMS)
    return o, new_state


def _hgrn_kernel(q_ref, f_ref, i_ref, g_ref, alb_ref, gn_ref, s0_ref, og_ref, sout_ref, s_ref,
                 *, layer, chunk, sub):
    t_step = pl.program_id(2)
    bb, tc, width = q_ref.shape
    hp = width // HEAD_DIM
    state_t = pl.num_programs(2) * (tc // chunk) > 1

    def relayout(dst_ref, src_ref):
        for bi in range(bb):
            for hh in range(hp):
                dst_ref[bi, hh] = jnp.transpose(src_ref[bi, hh]) if state_t else src_ref[bi, hh]

    @pl.when(t_step == 0)
    def _():
        relayout(s_ref, s0_ref)

    a = alb_ref[...]
    e = jnp.exp(a - jnp.max(a, axis=0, keepdims=True))
    sm = e / jnp.sum(e, axis=0, keepdims=True)
    lb = jnp.sum(sm[:layer + 1], axis=0, keepdims=True)

    def chunk_body(ci, carry):
        rows = pl.ds(pl.multiple_of(ci * chunk, chunk), chunk)
        for bi in range(bb):
            for hh in range(hp):
                cols = slice(hh * HEAD_DIM, (hh + 1) * HEAD_DIM)
                o, state = _hgrn_chunk(q_ref[bi, rows, cols], f_ref[bi, rows, cols], i_ref[bi, rows, cols],
                                       s_ref[bi, hh], lb[:, cols], sub, state_t)
                og_ref[bi, rows, cols] = _rms(o, gn_ref[...]) * _silu(g_ref[bi, rows, cols])
                s_ref[bi, hh] = state
        return carry

    lax.fori_loop(0, tc // chunk, chunk_body, 0)

    @pl.when(t_step == pl.num_programs(2) - 1)
    def _():
        relayout(sout_ref, s_ref)


def _hgrn(proj, a_lb, g_norm, s0, *, layer, bb, hp, tc):
    b, t, d4 = proj.shape
    d = d4 // 4
    h = d // HEAD_DIM
    chunk = min(HGRN_CHUNK, t)
    sub = min(HGRN_SUB, chunk)
    assert b % bb == 0 and h % hp == 0 and t % tc == 0 and tc % chunk == 0 and chunk % sub == 0
    nl = a_lb.shape[0]
    width = hp * HEAD_DIM
    hg = h // hp

    def col(off):
        return pl.BlockSpec((bb, tc, width), lambda bi, hi, ti: (bi, ti, hi + off * hg))

    state_shape = (bb, hp, HEAD_DIM, HEAD_DIM)
    state_spec = pl.BlockSpec(state_shape, lambda bi, hi, ti: (bi, hi, 0, 0))
    blocks = 5 * _nbytes((bb, tc, width), F32) + 2 * _nbytes(state_shape, F32)
    return pl.pallas_call(
        functools.partial(_hgrn_kernel, layer=layer, chunk=chunk, sub=sub),
        grid=(b // bb, hg, t // tc),
        in_specs=[
            col(0), col(1), col(2), col(3),
            pl.BlockSpec((nl, width), lambda bi, hi, ti: (0, hi)),
            pl.BlockSpec((1, HEAD_DIM), lambda bi, hi, ti: (0, 0)),
            state_spec,
        ],
        out_specs=[
            pl.BlockSpec((bb, tc, width), lambda bi, hi, ti: (bi, ti, hi)),
            state_spec,
        ],
        out_shape=[
            jax.ShapeDtypeStruct((b, t, d), F32),
            jax.ShapeDtypeStruct(s0.shape, F32),
        ],
        scratch_shapes=[pltpu.VMEM(state_shape, F32)],
        compiler_params=_params(("parallel", "parallel", "arbitrary"), blocks, _nbytes(state_shape, F32)),
        name="hgrn",
    )(proj, proj, proj, proj, a_lb, g_norm.reshape(1, HEAD_DIM), s0)


def _sb_prompt_kernel(bias_ref, q_ref, k_ref, v_ref, o_ref, *, bq, bk, scale):
    group = pl.program_id(1)
    qi = pl.program_id(2)
    hp = q_ref.shape[1] // HEAD_DIM
    nsub = bq // bk
    heads = [slice(hh * HEAD_DIM, (hh + 1) * HEAD_DIM) for hh in range(hp)]
    bias = [bias_ref[group * hp + hh] * LOG2E for hh in range(hp)]
    q = [(q_ref[:, cols] * (scale * LOG2E)).astype(BF16) for cols in heads]
    cw = bk
    row = lax.broadcasted_iota(jnp.int32, (cw, cw), 0)
    colm = lax.broadcasted_iota(jnp.int32, (cw, cw), 1)
    suffix = (row >= colm).astype(BF16)
    q_pos = lax.broadcasted_iota(jnp.int32, (bq, bk), 0)
    k_pos = lax.broadcasted_iota(jnp.int32, (bq, bk), 1)

    def visit(kb, carry, mask):
        rows = pl.ds(pl.multiple_of(kb * bk, bk), bk)
        out = []
        for hh, cols in enumerate(heads):
            acc, r = carry[hh]
            k = k_ref[rows, cols].astype(BF16)
            v = v_ref[rows, cols].astype(BF16)
            z = _dot(q[hh], k, NT_DIMS) + bias[hh]
            log_rest = -_softplus2(z)
            if mask is not None:
                log_rest = jnp.where(mask, log_rest, 0.0)
            lr = log_rest.astype(BF16)
            parts, newer = [], r
            for cb in reversed(range(bk // cw)):
                kcols = slice(cb * cw, (cb + 1) * cw)
                csum = _dot(lr[:, kcols], suffix) + newer
                parts.insert(0, csum)
                newer = csum[:, 0:1]
            a = jnp.exp2(z + (parts[0] if len(parts) == 1 else jnp.concatenate(parts, axis=1)))
            if mask is not None:
                a = jnp.where(mask, a, 0.0)
            out.append((acc + _dot(a.astype(BF16), v), newer))
        return tuple(out)

    carry = ((jnp.zeros((bq, HEAD_DIM), F32), jnp.zeros((bq, 1), F32)),) * hp
    for sub in reversed(range(nsub)):
        carry = visit(qi * nsub + sub, carry, k_pos + sub * bk < q_pos)
    carry = lax.fori_loop(0, qi * nsub, lambda step, c: visit(qi * nsub - 1 - step, c, None), carry)
    for hh, cols in enumerate(heads):
        o_ref[:, cols] = carry[hh][0].astype(o_ref.dtype)


def _sb_prompt(q, k, v, bias):
    b, t, d = q.shape
    bq, bk = min(SB_BLOCK_Q, t), min(SB_BLOCK_K, t)
    width = SB_HEADS_PER_STEP * HEAD_DIM
    assert t % bq == 0 and bq % bk == 0 and d % width == 0
    kv_spec = pl.BlockSpec((None, t, width), lambda bi, hi, qi: (bi, 0, hi))
    q_spec = pl.BlockSpec((None, bq, width), lambda bi, hi, qi: (bi, qi, hi))
    blocks = 2 * _nbytes((t, width), F32) + _nbytes((bq, width), F32) + _nbytes((bq, width), BF16)
    return pl.pallas_call(
        functools.partial(_sb_prompt_kernel, bq=bq, bk=bk, scale=HEAD_DIM ** -0.5),
        grid=(b, d // width, t // bq),
        in_specs=[pl.BlockSpec(memory_space=pltpu.SMEM), q_spec, kv_spec, kv_spec],
        out_specs=q_spec,
        out_shape=jax.ShapeDtypeStruct((b, t, d), BF16),
        compiler_params=_params(("parallel", "parallel", "arbitrary"), blocks),
        name="sb_prompt",
    )(bias, q, k, v)


def _slab_head(slab_ref, hh):
    keys, nh, dim = slab_ref.shape
    return slab_ref.reshape(keys * nh, dim)[pl.ds(hh, keys, stride=nh), :]


def _sb_sample_kernel(pt_ref, q_ref, kn_ref, vn_ref, bias_ref, *rest, pages, scale):
    nq, d = q_ref.shape
    h = d // HEAD_DIM
    groups = h // SUBLANES
    n_slabs = pages * groups
    k_refs, v_refs = rest[:n_slabs], rest[n_slabs:2 * n_slabs]
    o_ref, qbd_ref, kc_ref, vc_ref, acc_ref, r_ref = rest[2 * n_slabs:]
    step = pl.program_id(1)
    page = k_refs[0].shape[0]
    lanes = h * nq
    heads = [slice(hh * HEAD_DIM, (hh + 1) * HEAD_DIM) for hh in range(h)]

    def visit(n, mask):
        z = _dot(kc_ref[0:n, :], qbd_ref[...], NT_DIMS) + bias_ref[...]
        log_rest = -_softplus(z)
        if mask is not None:
            log_rest = jnp.where(mask, log_rest, 0.0)
        cw = min(SB_CUMSUM_BLOCK, n)
        ri = lax.broadcasted_iota(jnp.int32, (cw, cw), 0)
        ci = lax.broadcasted_iota(jnp.int32, (cw, cw), 1)
        suffix = (ci >= ri).astype(BF16)
        hi, lo = _bf16_split(log_rest)
        parts, newer = [], r_ref[...]
        for cb in reversed(range(n // cw)):
            krows = slice(cb * cw, (cb + 1) * cw)
            csum = _dot(suffix, hi[krows]) + _dot(suffix, lo[krows]) + newer
            parts.insert(0, csum)
            newer = csum[0:1, :]
        a = jnp.exp(z + (parts[0] if len(parts) == 1 else jnp.concatenate(parts, axis=0)))
        if mask is not None:
            a = jnp.where(mask, a, 0.0)
        acc_ref[...] += _dot(jnp.transpose(a).astype(BF16), vc_ref[0:n, :])
        r_ref[...] = newer

    @pl.when(step == 0)
    def _():
        qs = q_ref[...] * scale
        qt = jnp.broadcast_to(qs[None], (h, nq, d)).reshape(lanes, d)
        rr = lax.broadcasted_iota(jnp.int32, (lanes, d), 0)
        cc = lax.broadcasted_iota(jnp.int32, (lanes, d), 1)
        qbd_ref[...] = jnp.where(rr // nq == cc // HEAD_DIM, qt, 0.0).astype(BF16)
        acc_ref[...] = jnp.zeros_like(acc_ref)
        r_ref[...] = jnp.zeros_like(r_ref)
        pad = jnp.zeros((page - nq, d), BF16)
        kc_ref[0:page, :] = jnp.concatenate([kn_ref[...].astype(BF16), pad], axis=0)
        vc_ref[0:page, :] = jnp.concatenate([vn_ref[...].astype(BF16), pad], axis=0)
        key = lax.broadcasted_iota(jnp.int32, (page, lanes), 0)
        query = lax.broadcasted_iota(jnp.int32, (page, lanes), 1) % nq
        visit(page, key < query)

    @pl.when(step > 0)
    def _():
        for p in range(pages):
            rows = slice((pages - 1 - p) * page, (pages - p) * page)
            for g in range(groups):
                for hh in range(SUBLANES):
                    cols = heads[g * SUBLANES + hh]
                    kc_ref[rows, cols] = _slab_head(k_refs[p * groups + g], hh).astype(BF16)
                    vc_ref[rows, cols] = _slab_head(v_refs[p * groups + g], hh).astype(BF16)
        visit(pages * page, None)

    @pl.when(step == pl.num_programs(1) - 1)
    def _():
        for hh, cols in enumerate(heads):
            o_ref[:, cols] = acc_ref[hh * nq:(hh + 1) * nq, cols]


def _sb_sample(q, k_new, v_new, bias, cache_k, cache_v, page_table):
    b, nq, d = q.shape
    n_pool, page, h, dh = cache_k.shape
    n_pages = page_table.shape[1]
    pages = min(SB_PAGES_PER_STEP, n_pages)
    assert n_pages % pages == 0 and h * nq == HEAD_DIM and nq <= page and dh == HEAD_DIM and h * dh == d and h % SUBLANES == 0
    lanes = h * nq

    def slab_spec(p, g):
        def index(bi, si, pt):
            group = jnp.maximum(si - 1, 0)
            return (pt[bi * n_pages + n_pages - 1 - (group * pages + p)], 0, g, 0)
        return pl.BlockSpec((None, page, SUBLANES, dh), index)

    slab_specs = [slab_spec(p, g) for p in range(pages) for g in range(h // SUBLANES)]

    tok_spec = pl.BlockSpec((None, nq, d), lambda bi, si, pt: (bi, 0, 0))
    scratch = [
        pltpu.VMEM((lanes, d), BF16),
        pltpu.VMEM((pages * page, d), BF16),
        pltpu.VMEM((pages * page, d), BF16),
        pltpu.VMEM((lanes, d), F32),
        pltpu.VMEM((1, lanes), F32),
    ]
    blocks = 2 * pages * _nbytes((page, d), F32) + 4 * _nbytes((nq, d), F32)
    scratch_bytes = (_nbytes((lanes, d), BF16) + 2 * _nbytes((pages * page, d), BF16)
                     + _nbytes((lanes, d), F32))
    grid_spec = pltpu.PrefetchScalarGridSpec(
        num_scalar_prefetch=1,
        grid=(b, 1 + n_pages // pages),
        in_specs=[tok_spec, tok_spec, tok_spec,
                  pl.BlockSpec((1, lanes), lambda bi, si, pt: (0, 0))]
                 + slab_specs * 2,
        out_specs=tok_spec,
        scratch_shapes=scratch,
    )
    return pl.pallas_call(
        functools.partial(_sb_sample_kernel, pages=pages, scale=HEAD_DIM ** -0.5),
        grid_spec=grid_spec,
        out_shape=jax.ShapeDtypeStruct((b, nq, d), F32),
        compiler_params=_params(("parallel", "arbitrary"), blocks, scratch_bytes),
        name="sb_sample",
    )(page_table.reshape(-1), q, k_new, v_new, jnp.repeat(bias, nq).reshape(1, lanes),
      *([cache_k] * len(slab_specs)), *([cache_v] * len(slab_specs)))


def _run_trunk(x3, p, hgrn_state, conv_state, attend, w, *, hgrn_bb, hgrn_hp, hgrn_tc, fuse_conv):
    b, t, d = x3.shape
    m = b * t
    depth = w["norm_mix"].shape[0]
    n_a = len(w["a_w_in"])
    f = w["ffn_w_out"].shape[1]
    x = x3.reshape(m, d)
    new_s, new_c = [], []
    k_sh = v_sh = None
    for layer in range(depth):
        if layer < n_a:
            proj = _norm_matmul(x, w["norm_mix"][layer], w["a_w_in"][layer], n_out=4 * d)
            og, s = _hgrn(proj.reshape(b, t, 4 * d), w["a_lb"], w["a_g_norm"][layer], hgrn_state[layer],
                          layer=layer, bb=hgrn_bb, hp=hgrn_hp, tc=hgrn_tc)
            new_s.append(s)
            x = _matmul_res(og.reshape(m, d), w["a_w_out"][layer], x, tn=1024)
        else:
            bi = layer - n_a
            q = _norm_matmul(x, w["norm_mix"][layer], w["b_w_q"][bi], n_out=d, head_g=w["b_q_norm"][bi])
            o = attend(q.reshape(b, t, d), k_sh, v_sh, w["b_sb_bias"][bi])
            x = _matmul_res(o.reshape(m, d), w["b_w_out"][bi], x, tn=1024)
        if fuse_conv:
            act, cbuf = _ffn_in(x, w["norm_ffn"][layer], w["ffn_w_in"], conv_state[layer],
                                w["ffn_conv_w"][layer], w["ffn_conv_b"][layer], seq_len=t, layer=layer)
        else:
            u = _norm_matmul(x, w["norm_ffn"][layer], w["ffn_w_in"], n_out=2 * f, layer=layer)
            act, cbuf = _conv_gate(u.reshape(b, t, 2 * f), conv_state[layer], w["ffn_conv_w"][layer],
                                   w["ffn_conv_b"][layer], bb=b)
        new_c.append(cbuf)
        x = _matmul_res(act.reshape(m, f), w["ffn_w_out"], x, layer=layer)
        x = _ple(x, p.reshape(depth, m, -1), layer, w["ple_norm"][layer], w["ple_w"][layer],
                 w["ple_w_gate"][layer])
        if layer == n_a - 1:
            k_sh, v_sh = _kv_proj(x, w["kv_norm"], w["kv_w"], w["k_norm"])
            k_sh, v_sh = k_sh.reshape(b, t, d), v_sh.reshape(b, t, d)
    return x.reshape(b, t, d), k_sh, v_sh, jnp.stack(new_s), jnp.stack(new_c)


def kernel(x_prompt, x_sample, p_prompt, p_sample, cache_k, cache_v, page_table, state_hgrn, state_conv, norm_mix, norm_ffn, a_w_in, a_lb, a_g_norm, a_w_out, kv_norm, kv_w, k_norm, b_w_q, b_q_norm, b_sb_bias, b_w_out, ffn_w_in, ffn_conv_w, ffn_conv_b, ffn_w_out, ple_w, ple_norm, ple_w_gate):
    w = dict(norm_mix=norm_mix, norm_ffn=norm_ffn, a_lb=a_lb, a_g_norm=a_g_norm, kv_norm=kv_norm,
             k_norm=k_norm, b_q_norm=b_q_norm, b_sb_bias=b_sb_bias, ffn_conv_w=ffn_conv_w,
             ffn_conv_b=ffn_conv_b, ple_norm=ple_norm)
    for name, val in dict(a_w_in=a_w_in, a_w_out=a_w_out, b_w_q=b_w_q, b_w_out=b_w_out, ple_w=ple_w,
                          ple_w_gate=ple_w_gate).items():
        w[name] = [val[layer].astype(BF16) for layer in range(val.shape[0])]
    for name, val in dict(kv_w=kv_w, ffn_w_in=ffn_w_in, ffn_w_out=ffn_w_out).items():
        w[name] = val.astype(BF16)

    b, t, d = x_prompt.shape
    heads = d // HEAD_DIM
    n_a = a_w_in.shape[0]
    depth = norm_mix.shape[0]
    f = ffn_w_out.shape[1]
    zero_h = jnp.zeros((n_a, b, heads, HEAD_DIM, HEAD_DIM), state_hgrn.dtype)
    zero_c = jnp.zeros((depth, b, state_conv.shape[2], f), state_conv.dtype)

    def attend_prompt(q, k, v, bias):
        return _sb_prompt(q, k, v, bias)

    def attend_sample(q, k, v, bias):
        return _sb_sample(q, k, v, bias, cache_k, cache_v, page_table)

    y_p, k_p, v_p, h_p, c_p = _run_trunk(x_prompt, p_prompt, zero_h, zero_c, attend_prompt, w,
                                         hgrn_bb=1, hgrn_hp=8, hgrn_tc=512, fuse_conv=True)
    y_s, k_s, v_s, h_s, c_s = _run_trunk(x_sample, p_sample, state_hgrn, state_conv, attend_sample, w,
                                         hgrn_bb=8, hgrn_hp=1, hgrn_tc=x_sample.shape[1], fuse_conv=False)
    hs = (heads, HEAD_DIM)
    return (y_p, y_s,
            k_p.reshape(k_p.shape[:2] + hs), v_p.reshape(v_p.shape[:2] + hs),
            k_s.reshape(k_s.shape[:2] + hs), v_s.reshape(v_s.shape[:2] + hs),
            h_p, h_s, c_p, c_s)
```
